```python
import math
import jax, jax.numpy as jnp
from jax import lax
import numpy as np

D_MODEL = 1024
BATCH = 2
SEQ = 8192
DEPTH = 4

BRANCH_WIDTH = D_MODEL // 2
N_BRANCHES = 3
GDN_HEAD_DIM = 128
GDN_HEADS = BRANCH_WIDTH // GDN_HEAD_DIM
GDN_CONV = 4
GDN_CHUNK = 64
DIFF_V_DIM = 128
DIFF_QK_DIM = DIFF_V_DIM // 2
DIFF_HEADS = BRANCH_WIDTH // DIFF_V_DIM
Q_BLOCK = 128
SC_CONV = 3
NORM_EPS = 1e-6
NEG_INF = -1e30
SPLIT_SIZES = (3 * BRANCH_WIDTH, BRANCH_WIDTH, GDN_HEADS, GDN_HEADS,
               BRANCH_WIDTH, BRANCH_WIDTH, BRANCH_WIDTH, BRANCH_WIDTH,
               BRANCH_WIDTH, BRANCH_WIDTH, BRANCH_WIDTH, BRANCH_WIDTH,
               N_BRANCHES * D_MODEL)
N_IN = sum(SPLIT_SIZES)

kernel_name = 'hybrid_gdn_diffattn_shortconv_block'


def rms_norm(x, w, eps=NORM_EPS):
    x32 = x.astype(jnp.float32)
    y = x32 * lax.rsqrt(jnp.mean(x32 * x32, axis=-1, keepdims=True) + eps)
    return (y * w.astype(jnp.float32)).astype(x.dtype)


def l2_normalize(x, eps=1e-6):
    return x * lax.rsqrt(jnp.sum(x * x, axis=-1, keepdims=True) + eps)


def causal_dwconv(x, w):
    k_width, channels = w.shape
    return lax.conv_general_dilated(
        x, w[:, None, :].astype(x.dtype), window_strides=(1,),
        padding=[(k_width - 1, 0)], dimension_numbers=('NWC', 'WIO', 'NWC'),
        feature_group_count=channels)


def alibi_slopes(n_heads):
    return 2.0 ** (-8.0 * jnp.arange(1, n_heads + 1, dtype=jnp.float32) / n_heads)


def chunk_gated_delta_rule(q, k, v, g, beta):
    bsz, seq, n_h, dk = q.shape
    dv = v.shape[-1]
    n_chunks = seq // GDN_CHUNK

    def to_chunks(t):
        t = jnp.moveaxis(t, 2, 1)
        return t.reshape(t.shape[:2] + (n_chunks, GDN_CHUNK) + t.shape[3:])

    q = to_chunks(q * dk ** -0.5)
    k = to_chunks(k)
    v = to_chunks(v)
    beta = to_chunks(beta)
    g = jnp.cumsum(to_chunks(g), axis=-1)
    k_beta = k * beta[..., None]
    v_beta = v * beta[..., None]
    tril = jnp.tril(jnp.ones((GDN_CHUNK, GDN_CHUNK), bool))
    strict = jnp.tril(jnp.ones((GDN_CHUNK, GDN_CHUNK), bool), k=-1)
    decay = jnp.exp(jnp.where(tril, g[..., :, None] - g[..., None, :], -jnp.inf))
    lower = jnp.where(strict, jnp.einsum('bhncd,bhnsd->bhncs', k_beta, k) * decay, 0.0)
    unit_lower = lower + jnp.eye(GDN_CHUNK, dtype=lower.dtype)
    rhs = jnp.concatenate([v_beta, k_beta * jnp.exp(g)[..., None]], axis=-1)
    sol = lax.linalg.triangular_solve(unit_lower, rhs, left_side=True, lower=True,
                                      unit_diagonal=True)
    u, w = sol[..., :dv], sol[..., dv:]
    qk = jnp.where(tril, jnp.einsum('bhncd,bhnsd->bhncs', q, k) * decay, 0.0)
    q_dec = q * jnp.exp(g)[..., None]
    k_dec = k * jnp.exp(g[..., -1:] - g)[..., None]
    g_last = jnp.exp(g[..., -1])

    def step(state, inp):
        u_c, w_c, qk_c, q_dec_c, k_dec_c, g_last_c = inp
        v_new = u_c - w_c @ state
        o_c = q_dec_c @ state + qk_c @ v_new
        state = state * g_last_c[..., None, None] + jnp.einsum('bhcd,bhce->bhde', k_dec_c, v_new)
        return state, o_c

    xs = [jnp.moveaxis(t, 2, 0) for t in (u, w, qk, q_dec, k_dec, g_last)]
    state0 = jnp.zeros((bsz, n_h, dk, dv), jnp.float32)
    _, o = lax.scan(step, state0, tuple(xs))
    o = jnp.moveaxis(o, 0, 2).reshape(bsz, n_h, seq, dv)
    return jnp.moveaxis(o, 1, 2)


def gdn_branch(qkv_pre, z, b, a, conv_w, a_log, dt_bias, norm_w):
    bsz, seq, _ = qkv_pre.shape
    qkv = jax.nn.silu(causal_dwconv(qkv_pre, conv_w)).astype(jnp.float32)
    q, k, v = jnp.split(qkv, 3, axis=-1)
    shp = (bsz, seq, GDN_HEADS, GDN_HEAD_DIM)
    q = l2_normalize(q.reshape(shp))
    k = l2_normalize(k.reshape(shp))
    v = v.reshape(shp)
    beta = jax.nn.sigmoid(b.astype(jnp.float32))
    g = -jnp.exp(a_log.astype(jnp.float32)) * jax.nn.softplus(
        a.astype(jnp.float32) + dt_bias.astype(jnp.float32))
    o = chunk_gated_delta_rule(q, k, v, g, beta)
    o = rms_norm(o.astype(z.dtype), norm_w).reshape(bsz, seq, BRANCH_WIDTH)
    return o * jax.nn.silu(z)


def diff_attn_branch(q, k, v, z, lam_vecs, lam_init, norm_w):
    bsz, seq, _ = q.shape
    q = (q * DIFF_QK_DIM ** -0.5).reshape(bsz, seq, DIFF_HEADS, 2, DIFF_QK_DIM)
    k = k.reshape(bsz, seq, DIFF_HEADS, 2, DIFF_QK_DIM)
    v = v.reshape(bsz, seq, DIFF_HEADS, DIFF_V_DIM)
    lv = lam_vecs.astype(jnp.float32)
    lam = jnp.exp(jnp.sum(lv[0] * lv[1])) - jnp.exp(jnp.sum(lv[2] * lv[3])) + lam_init
    slopes = alibi_slopes(DIFF_HEADS)
    n_blk = seq // Q_BLOCK
    q_blocks = jnp.moveaxis(q.reshape(bsz, n_blk, Q_BLOCK, DIFF_HEADS, 2, DIFF_QK_DIM), 1, 0)
    k_pos = jnp.arange(seq)

    def attend(args):
        q_blk, blk = args
        s = jnp.einsum('bqhmd,bkhmd->bhmqk', q_blk, k).astype(jnp.float32)
        dist = (blk * Q_BLOCK + jnp.arange(Q_BLOCK))[:, None] - k_pos[None, :]
        s = jnp.where(dist >= 0, s - slopes[:, None, None, None] * dist.astype(jnp.float32), NEG_INF)
        p = jax.nn.softmax(s, axis=-1)
        p = p[:, :, 0] - lam * p[:, :, 1]
        return jnp.einsum('bhqk,bkhe->bqhe', p.astype(v.dtype), v)

    o = lax.map(attend, (q_blocks, jnp.arange(n_blk)))
    o = jnp.moveaxis(o, 0, 1).reshape(bsz, seq, DIFF_HEADS, DIFF_V_DIM)
    o = (rms_norm(o, norm_w) * (1.0 - lam_init)).reshape(bsz, seq, BRANCH_WIDTH)
    return o * jax.nn.silu(z)


def short_conv_branch(b, c, x_in, z, conv_w):
    y = b * causal_dwconv(c * x_in, conv_w)
    return y * jax.nn.silu(z)


def setup_inputs(seed: int = 0) -> dict:
    key = jax.random.key(seed)
    ks = jax.random.split(key, 14)
    f32 = jnp.float32
    nrm = jax.random.normal
    x = nrm(ks[0], (BATCH, SEQ, D_MODEL), f32)
    norm_pre = 1.0 + 0.1 * nrm(ks[1], (DEPTH, D_MODEL), f32)
    norm_post = 1.0 + 0.1 * nrm(ks[2], (DEPTH, D_MODEL), f32)
    w_in = nrm(ks[3], (DEPTH, D_MODEL, N_IN), f32) * D_MODEL ** -0.5
    gdn_conv_w = nrm(ks[4], (DEPTH, GDN_CONV, 3 * BRANCH_WIDTH), f32) * GDN_CONV ** -0.5
    gdn_a_log = jnp.log(jax.random.uniform(ks[5], (DEPTH, GDN_HEADS), f32, 1.0, 16.0))
    dt = jnp.exp(jax.random.uniform(ks[6], (DEPTH, GDN_HEADS), f32, math.log(1e-3), math.log(1e-1)))
    gdn_dt_bias = dt + jnp.log(-jnp.expm1(-dt))
    gdn_norm_w = 1.0 + 0.1 * nrm(ks[7], (DEPTH, GDN_HEAD_DIM), f32)
    diff_lambda = 0.1 * nrm(ks[8], (DEPTH, 4, DIFF_QK_DIM), f32)
    diff_norm_w = 1.0 + 0.1 * nrm(ks[9], (DEPTH, DIFF_V_DIM), f32)
    sc_conv_w = nrm(ks[10], (DEPTH, SC_CONV, BRANCH_WIDTH), f32) * SC_CONV ** -0.5
    w_branch = nrm(ks[11], (DEPTH, N_BRANCHES, BRANCH_WIDTH, D_MODEL), f32) * BRANCH_WIDTH ** -0.5
    w_out = nrm(ks[12], (DEPTH, D_MODEL, D_MODEL), f32) * D_MODEL ** -0.5
    return {'x': x, 'norm_pre': norm_pre, 'norm_post': norm_post, 'w_in': w_in,
            'gdn_conv_w': gdn_conv_w, 'gdn_a_log': gdn_a_log, 'gdn_dt_bias': gdn_dt_bias,
            'gdn_norm_w': gdn_norm_w, 'diff_lambda': diff_lambda, 'diff_norm_w': diff_norm_w,
            'sc_conv_w': sc_conv_w, 'w_branch': w_branch, 'w_out': w_out}


def reference(x, norm_pre, norm_post, w_in, gdn_conv_w, gdn_a_log, gdn_dt_bias, gdn_norm_w,
              diff_lambda, diff_norm_w, sc_conv_w, w_branch, w_out):
    bsz, seq, _ = x.shape
    split_at = [int(i) for i in np.cumsum(SPLIT_SIZES)[:-1]]
    for l in range(DEPTH):
        h = rms_norm(x, norm_pre[l])
        proj = jnp.einsum('bld,dn->bln', h, w_in[l])
        (gdn_qkv, gdn_z, gdn_b, gdn_a, d_q, d_k, d_v, d_z,
         s_b, s_c, s_x, s_z, merge_logits) = jnp.split(proj, split_at, axis=-1)
        lam_init = 0.8 - 0.6 * math.exp(-0.3 * l)
        y_a = gdn_branch(gdn_qkv, gdn_z, gdn_b, gdn_a, gdn_conv_w[l], gdn_a_log[l],
                         gdn_dt_bias[l], gdn_norm_w[l])
        y_b = diff_attn_branch(d_q, d_k, d_v, d_z, diff_lambda[l], lam_init, diff_norm_w[l])
        y_c = short_conv_branch(s_b, s_c, s_x, s_z, sc_conv_w[l])
        branches = jnp.stack([y_a, y_b, y_c], axis=2)
        branch_out = jnp.einsum('blkw,kwd->blkd', branches, w_branch[l])
        gates = jax.nn.sigmoid(merge_logits.reshape(bsz, seq, N_BRANCHES, D_MODEL))
        merged = jnp.sum(gates * branch_out, axis=2)
        out = jnp.einsum('bld,de->ble', merged, w_out[l])
        x = x + rms_norm(out, norm_post[l])
    return x
```

```python
import functools
import math

import jax
import jax.numpy as jnp
from jax import lax
from jax.experimental import pallas as pl
from jax.experimental.pallas import tpu as pltpu

F32 = jnp.float32
BF16 = jnp.bfloat16

LANES = 128
SUBLANES = 8
VMEM_LIMIT = 56 * 1024 * 1024

BRANCH_WIDTH = 512
HEAD_DIM = 128
N_HEADS = BRANCH_WIDTH // HEAD_DIM
DIFF_QK_DIM = 64
GDN_CONV = 4
SC_CONV = 3
NORM_EPS = 1e-6
L2_EPS = 1e-6
NEG_BIG = -1e30
INV_BASE = 16

PROJ_TILE = 512
GDN_CHUNK = 128
ATTN_Q_TILE = 256
ATTN_K_TILE = 256
FINAL_TILE = 256


def _dot(a, b):
    return jnp.dot(a.astype(BF16), b.astype(BF16), preferred_element_type=F32)


def _dot_nt(a, b):
    return lax.dot_general(a.astype(BF16), b.astype(BF16), (((1,), (1,)), ((), ())),
                           preferred_element_type=F32)


def _rms(x, w):
    return x * lax.rsqrt(jnp.mean(x * x, axis=-1, keepdims=True) + NORM_EPS) * w


def _sigmoid(x):
    return 1.0 / (1.0 + jnp.exp(-x))


def _silu(x):
    return x * _sigmoid(x)


def _softplus(x):
    return jnp.maximum(x, 0.0) + jnp.log(1.0 + jnp.exp(-jnp.abs(x)))


def _proj_kernel(x_ref, nw_ref, w_ref, wba_ref, p_ref, ba_ref, *, col_chunk):
    h = _rms(x_ref[...], nw_ref[...])
    hb = h.astype(BF16)
    for c in range(w_ref.shape[1] // col_chunk):
        cols = slice(c * col_chunk, (c + 1) * col_chunk)
        p_ref[:, cols] = jnp.dot(hb, w_ref[:, cols], preferred_element_type=F32).astype(BF16)
    ba_ref[...] = jnp.dot(h, wba_ref[...], preferred_element_type=F32,
                          precision=lax.Precision.HIGHEST)


def _proj_call(x, norm_w, w1, wba, tile):
    n_tok, d = x.shape
    n_out = w1.shape[1]
    return pl.pallas_call(
        functools.partial(_proj_kernel, col_chunk=512),
        grid=(n_tok // tile,),
        in_specs=[pl.BlockSpec((tile, d), lambda i: (i, 0)),
                  pl.BlockSpec((1, d), lambda i: (0, 0)),
                  pl.BlockSpec((d, n_out), lambda i: (0, 0)),
                  pl.BlockSpec((d, LANES), lambda i: (0, 0))],
        out_specs=[pl.BlockSpec((tile, n_out), lambda i: (i, 0)),
                   pl.BlockSpec((tile, LANES), lambda i: (i, 0))],
        out_shape=[jax.ShapeDtypeStruct((n_tok, n_out), BF16),
                   jax.ShapeDtypeStruct((n_tok, LANES), F32)],
        compiler_params=pltpu.CompilerParams(dimension_semantics=("arbitrary",),
                                             vmem_limit_bytes=VMEM_LIMIT),
        name="proj",
    )(x, norm_w, w1, wba)


def _unit_lower_inverse(lm, xr):
    n = lm.shape[0]
    eye = jnp.where(xr == 0, 1.0, 0.0).astype(F32)
    nb = jnp.where(xr < INV_BASE, -lm, 0.0)
    t = eye + nb
    power = nb
    step = 2
    while step < INV_BASE:
        power = _dot(power, power)
        t = t + _dot(t, power)
        step *= 2
    size = INV_BASE
    while size < n:
        off = jnp.where((xr >= size) & (xr < 2 * size), lm, 0.0)
        t = t - _dot(t, _dot(off, t))
        size *= 2
    return t


def _gdn_kernel(p_ref, ba_ref, cw_ref, alog_ref, dtb_ref, nw_ref, o_ref, xbuf, state, *, chunk):
    t_idx = pl.program_id(1)
    halo = SUBLANES

    @pl.when(t_idx == 0)
    def _():
        xbuf[0:halo, :] = jnp.zeros((halo, xbuf.shape[1]), F32)
        state[...] = jnp.zeros(state.shape, F32)

    xbuf[halo:, :] = p_ref[...].astype(F32)

    pre = ba_ref[...]
    lane = lax.broadcasted_iota(jnp.int32, pre.shape, 1)
    g_all = -jnp.exp(alog_ref[...]) * _softplus(pre + dtb_ref[...])
    gb = jnp.where(lane < N_HEADS, _sigmoid(pre), jnp.where(lane < 2 * N_HEADS, g_all, 0.0))

    ri = lax.broadcasted_iota(jnp.int32, (chunk, chunk), 0)
    ci = lax.broadcasted_iota(jnp.int32, (chunk, chunk), 1)
    tril = ri >= ci
    strict = ri > ci
    xr = ri ^ ci
    gc = jnp.dot(jnp.where(tril, 1.0, 0.0).astype(F32), gb, preferred_element_type=F32,
                 precision=lax.Precision.HIGHEST)
    gc_t = gc.T

    def conv_silu(col):
        acc = cw_ref[GDN_CONV - 1:GDN_CONV, col:col + HEAD_DIM] * xbuf[halo:halo + chunk, col:col + HEAD_DIM]
        for tap in range(GDN_CONV - 1):
            shift = GDN_CONV - 1 - tap
            acc = acc + (cw_ref[tap:tap + 1, col:col + HEAD_DIM]
                         * xbuf[halo - shift:halo - shift + chunk, col:col + HEAD_DIM])
        return _silu(acc)

    def l2n(v):
        return v * lax.rsqrt(jnp.sum(v * v, axis=-1, keepdims=True) + L2_EPS)

    for h in range(N_HEADS):
        q = l2n(conv_silu(h * HEAD_DIM)) * (HEAD_DIM ** -0.5)
        k = l2n(conv_silu(BRANCH_WIDTH + h * HEAD_DIM))
        v = conv_silu(2 * BRANCH_WIDTH + h * HEAD_DIM)
        beta = gb[:, h:h + 1]
        gcol = gc[:, N_HEADS + h:N_HEADS + h + 1]
        grow = gc_t[N_HEADS + h:N_HEADS + h + 1, :]
        glast = gc[chunk - 1:chunk, N_HEADS + h:N_HEADS + h + 1]
        decay = jnp.exp(jnp.where(tril, gcol - grow, NEG_BIG))
        kb = k * beta
        lm = jnp.where(strict, _dot_nt(kb, k) * decay, 0.0)
        t_inv = _unit_lower_inverse(lm, xr)
        eg = jnp.exp(gcol)
        sol = _dot(t_inv, jnp.concatenate([v * beta, kb * eg], axis=1))
        u, w = sol[:, :HEAD_DIM], sol[:, HEAD_DIM:]
        qk = _dot_nt(q, k) * decay
        k_dec = k * jnp.exp(glast - gcol)
        s_old = state[h]
        ws = _dot(jnp.concatenate([w, q * eg], axis=0), s_old)
        v_new = u - ws[:chunk]
        o = ws[chunk:] + _dot(qk, v_new)
        state[h] = s_old * jnp.exp(glast) + _dot(k_dec.T, v_new)
        o_ref[:, h * HEAD_DIM:(h + 1) * HEAD_DIM] = _rms(o, nw_ref[...]).astype(BF16)

    xbuf[0:halo, :] = xbuf[chunk:chunk + halo, :]


def _gdn_call(p1, ba, conv_w, alog_row, dtb_row, norm_w, batch, chunk):
    n_tok = p1.shape[0]
    seq = n_tok // batch
    n_t = seq // chunk
    width = 3 * BRANCH_WIDTH
    return pl.pallas_call(
        functools.partial(_gdn_kernel, chunk=chunk),
        grid=(batch, n_t),
        in_specs=[pl.BlockSpec((chunk, width), lambda b, t: (b * n_t + t, 0)),
                  pl.BlockSpec((chunk, LANES), lambda b, t: (b * n_t + t, 0)),
                  pl.BlockSpec((GDN_CONV, width), lambda b, t: (0, 0)),
                  pl.BlockSpec((1, LANES), lambda b, t: (0, 0)),
                  pl.BlockSpec((1, LANES), lambda b, t: (0, 0)),
                  pl.BlockSpec((1, HEAD_DIM), lambda b, t: (0, 0))],
        out_specs=pl.BlockSpec((chunk, BRANCH_WIDTH), lambda b, t: (b * n_t + t, 0)),
        out_shape=jax.ShapeDtypeStruct((n_tok, BRANCH_WIDTH), BF16),
        scratch_shapes=[pltpu.VMEM((chunk + SUBLANES, width), F32),
                        pltpu.VMEM((N_HEADS, HEAD_DIM, HEAD_DIM), F32)],
        compiler_params=pltpu.CompilerParams(dimension_semantics=("arbitrary", "arbitrary"),
                                             vmem_limit_bytes=VMEM_LIMIT),
        name="gdn",
    )(p1, ba, conv_w, alog_row, dtb_row, norm_w)


def _attn_kernel(q_ref, k_ref, v_ref, lam_ref, nw_ref, o_ref, m_scr, l_scr, acc_scr,
                 *, tq, tk, lam_init):
    head = pl.program_id(1)
    qi = pl.program_id(2)
    slope = jnp.float32(2.0 ** (-8.0 / N_HEADS))
    for hh in range(1, N_HEADS):
        slope = jnp.where(head == hh, jnp.float32(2.0 ** (-8.0 * (hh + 1) / N_HEADS)), slope)

    q = q_ref[...] * (DIFF_QK_DIM ** -0.5)
    lane = lax.broadcasted_iota(jnp.int32, q.shape, 1)
    zero = jnp.zeros_like(q)
    qq = jnp.concatenate([jnp.where(lane < DIFF_QK_DIM, q, zero),
                          jnp.where(lane >= DIFF_QK_DIM, q, zero)], axis=0)

    m_scr[...] = jnp.full(m_scr.shape, NEG_BIG, F32)
    l_scr[...] = jnp.zeros(l_scr.shape, F32)
    acc_scr[...] = jnp.zeros(acc_scr.shape, F32)

    col = lax.broadcasted_iota(jnp.int32, (1, tk), 1)
    row = lax.broadcasted_iota(jnp.int32, (2 * tq, tk), 0)
    row = jnp.where(row >= tq, row - tq, row)
    col_full = lax.broadcasted_iota(jnp.int32, (2 * tq, tk), 1)

    def tile(j, masked):
        start = pl.multiple_of(j * tk, tk)
        kt = k_ref[pl.ds(start, tk), :]
        vt = v_ref[pl.ds(start, tk), :]
        s = lax.dot_general(qq, kt, (((1,), (1,)), ((), ())), preferred_element_type=F32)
        s = s + slope * (col + (j * tk - qi * tq)).astype(F32)
        if masked:
            s = jnp.where(col_full + (j * tk - qi * tq) <= row, s, NEG_BIG)
        m_prev = m_scr[...]
        m_new = jnp.maximum(m_prev, jnp.max(s, axis=-1, keepdims=True))
        alpha = jnp.exp(m_prev - m_new)
        p = jnp.exp(s - m_new)
        l_scr[...] = alpha * l_scr[...] + jnp.sum(p, axis=-1, keepdims=True)
        acc_scr[...] = alpha * acc_scr[...] + jnp.dot(p.astype(BF16), vt, preferred_element_type=F32)
        m_scr[...] = m_new

    def body(j, carry):
        tile(j, False)
        return carry

    n_full = (qi * tq) // tk
    lax.fori_loop(0, n_full, body, 0)
    for d in range(tq // tk):
        tile(n_full + d, True)

    lv = lam_ref[...]
    lam = (jnp.exp(jnp.sum(lv[0:1] * lv[1:2], axis=-1, keepdims=True))
           - jnp.exp(jnp.sum(lv[2:3] * lv[3:4], axis=-1, keepdims=True)) + lam_init)
    o_all = acc_scr[...] / l_scr[...]
    o = o_all[:tq] - lam * o_all[tq:]
    o_ref[...] = (_rms(o, nw_ref[...]) * (1.0 - lam_init)).astype(BF16)


def _attn_call(p1, lam_vecs, norm_w, lam_init, batch, tq, tk):
    n_tok = p1.shape[0]
    seq = n_tok // batch
    n_q = seq // tq
    qcol = 3 * BRANCH_WIDTH // HEAD_DIM
    kcol = qcol + N_HEADS
    vcol = kcol + N_HEADS
    return pl.pallas_call(
        functools.partial(_attn_kernel, tq=tq, tk=tk, lam_init=lam_init),
        grid=(batch, N_HEADS, n_q),
        in_specs=[pl.BlockSpec((tq, HEAD_DIM), lambda b, h, i: (b * n_q + i, qcol + h)),
                  pl.BlockSpec((seq, HEAD_DIM), lambda b, h, i: (b, kcol + h)),
                  pl.BlockSpec((seq, HEAD_DIM), lambda b, h, i: (b, vcol + h)),
                  pl.BlockSpec((4, DIFF_QK_DIM), lambda b, h, i: (0, 0)),
                  pl.BlockSpec((1, HEAD_DIM), lambda b, h, i: (0, 0))],
        out_specs=pl.BlockSpec((tq, HEAD_DIM), lambda b, h, i: (b * n_q + i, h)),
        out_shape=jax.ShapeDtypeStruct((n_tok, BRANCH_WIDTH), BF16),
        scratch_shapes=[pltpu.VMEM((2 * tq, 1), F32),
                        pltpu.VMEM((2 * tq, 1), F32),
                        pltpu.VMEM((2 * tq, HEAD_DIM), F32)],
        compiler_params=pltpu.CompilerParams(
            dimension_semantics=("arbitrary", "arbitrary", "arbitrary"),
            vmem_limit_bytes=VMEM_LIMIT),
        name="diff_attn",
    )(p1, p1, p1, lam_vecs, norm_w)


def _final_kernel(x_ref, xh_ref, og_ref, oa_ref, npre_ref, npost_ref, w2_ref, scw_ref, wb_ref,
                  wo_ref, out_ref, *, tiles_per_seq):
    i = pl.program_id(0)
    w = BRANCH_WIDTH
    x = x_ref[...]
    hb = _rms(x, npre_ref[...]).astype(BF16)
    hh = _rms(xh_ref[...], npre_ref[...]).astype(BF16)
    tile = x.shape[0]

    def proj(cols):
        return jnp.dot(hb, w2_ref[:, cols], preferred_element_type=F32)

    cx = proj(slice(w, 2 * w)) * proj(slice(2 * w, 3 * w))
    cxh = (jnp.dot(hh, w2_ref[:, w:2 * w], preferred_element_type=F32)
           * jnp.dot(hh, w2_ref[:, 2 * w:3 * w], preferred_element_type=F32))
    cxh = jnp.where(i % tiles_per_seq == 0, 0.0, cxh)
    rows = lax.broadcasted_iota(jnp.int32, (tile, w), 0)
    conv = scw_ref[SC_CONV - 1:SC_CONV, :] * cx
    for tap in range(SC_CONV - 1):
        shift = SC_CONV - 1 - tap
        shifted = pltpu.roll(cx, shift, axis=0)
        for r in range(shift):
            shifted = jnp.where(rows == r, cxh[SUBLANES - shift + r:SUBLANES - shift + r + 1, :], shifted)
        conv = conv + scw_ref[tap:tap + 1, :] * shifted
    y_c = proj(slice(0, w)) * conv * _silu(proj(slice(5 * w, 6 * w)))
    y_a = og_ref[...].astype(F32) * _silu(proj(slice(3 * w, 4 * w)))
    y_b = oa_ref[...].astype(F32) * _silu(proj(slice(4 * w, 5 * w)))

    d = x.shape[1]
    merged = jnp.zeros((tile, d), F32)
    for k, y in enumerate((y_a, y_b, y_c)):
        gate = _sigmoid(proj(slice(6 * w + k * d, 6 * w + (k + 1) * d)))
        merged = merged + gate * jnp.dot(y.astype(BF16), wb_ref[k], preferred_element_type=F32)
    out = jnp.dot(merged.astype(BF16), wo_ref[...], preferred_element_type=F32)
    out_ref[...] = x + _rms(out, npost_ref[...])


def _final_call(x, o_gdn, o_attn, norm_pre, norm_post, w2, sc_w, w_branch, w_out, batch, tile):
    n_tok, d = x.shape
    tiles_per_seq = n_tok // batch // tile
    halo_blocks = tile // SUBLANES
    const = dict(pipeline_mode=pl.Buffered(1))
    return pl.pallas_call(
        functools.partial(_final_kernel, tiles_per_seq=tiles_per_seq),
        grid=(n_tok // tile,),
        in_specs=[pl.BlockSpec((tile, d), lambda i: (i, 0)),
                  pl.BlockSpec((SUBLANES, d), lambda i: (jnp.maximum(i * halo_blocks - 1, 0), 0)),
                  pl.BlockSpec((tile, BRANCH_WIDTH), lambda i: (i, 0)),
                  pl.BlockSpec((tile, BRANCH_WIDTH), lambda i: (i, 0)),
                  pl.BlockSpec((1, d), lambda i: (0, 0)),
                  pl.BlockSpec((1, d), lambda i: (0, 0)),
                  pl.BlockSpec(w2.shape, lambda i: (0, 0), **const),
                  pl.BlockSpec(sc_w.shape, lambda i: (0, 0)),
                  pl.BlockSpec(w_branch.shape, lambda i: (0, 0, 0), **const),
                  pl.BlockSpec(w_out.shape, lambda i: (0, 0), **const)],
        out_specs=pl.BlockSpec((tile, d), lambda i: (i, 0)),
        out_shape=jax.ShapeDtypeStruct((n_tok, d), F32),
        compiler_params=pltpu.CompilerParams(dimension_semantics=("arbitrary",),
                                             vmem_limit_bytes=VMEM_LIMIT),
        name="merge_out",
    )(x, x, o_gdn, o_attn, norm_pre, norm_post, w2, sc_w, w_branch, w_out)


def _split_w_in(w_in_l):
    w = BRANCH_WIDTH
    o = 0
    parts = {}
    for name, size in (("gdn_qkv", 3 * w), ("gdn_z", w), ("gdn_b", N_HEADS), ("gdn_a", N_HEADS),
                       ("d_q", w), ("d_k", w), ("d_v", w), ("d_z", w),
                       ("s_b", w), ("s_c", w), ("s_x", w), ("s_z", w), ("merge", None)):
        size = w_in_l.shape[1] - o if size is None else size
        parts[name] = w_in_l[:, o:o + size]
        o += size
    w1 = jnp.concatenate([parts[n] for n in ("gdn_qkv", "d_q", "d_k", "d_v")], axis=1).astype(BF16)
    wba = jnp.concatenate([parts["gdn_b"], parts["gdn_a"]], axis=1)
    wba = jnp.pad(wba, ((0, 0), (0, LANES - wba.shape[1])))
    w2 = jnp.concatenate([parts[n] for n in ("s_b", "s_c", "s_x", "gdn_z", "d_z", "s_z", "merge")],
                         axis=1).astype(BF16)
    return w1, wba, w2


def _head_row(vec):
    return jnp.pad(vec.astype(F32), (N_HEADS, LANES - 2 * N_HEADS)).reshape(1, LANES)


def kernel(x, norm_pre, norm_post, w_in, gdn_conv_w, gdn_a_log, gdn_dt_bias, gdn_norm_w,
           diff_lambda, diff_norm_w, sc_conv_w, w_branch, w_out):
    bsz, seq, d = x.shape
    depth = w_in.shape[0]
    xf = x.reshape(bsz * seq, d)
    for l in range(depth):
        w1, wba, w2 = _split_w_in(w_in[l])
        lam_init = 0.8 - 0.6 * math.exp(-0.3 * l)
        p1, ba = _proj_call(xf, norm_pre[l].reshape(1, d), w1, wba, min(PROJ_TILE, seq))
        o_gdn = _gdn_call(p1, ba, gdn_conv_w[l], _head_row(gdn_a_log[l]), _head_row(gdn_dt_bias[l]),
                          gdn_norm_w[l].reshape(1, HEAD_DIM), bsz, min(GDN_CHUNK, seq))
        o_attn = _attn_call(p1, diff_lambda[l], diff_norm_w[l].reshape(1, HEAD_DIM), lam_init, bsz,
                            min(ATTN_Q_TILE, seq), min(ATTN_K_TILE, seq))
        xf = _final_call(xf, o_gdn, o_attn, norm_pre[l].reshape(1, d), norm_post[l].reshape(1, d),
                         w2, sc_conv_w[l], w_branch[l].astype(BF16), w_out[l].astype(BF16), bsz,
                         min(FINAL_TILE, seq))
    return xf.reshape(bsz, seq, d)
```

```python
import functools
import math

import numpy as np
import jax
import jax.numpy as jnp
from jax import lax
from jax.experimental import pallas as pl
from jax.experimental.pallas import tpu as pltpu

F32 = jnp.float32
BF16 = jnp.bfloat16

LANES = 128
SUBLANES = 8
PACKED_SUBLANES = 16
VMEM_LIMIT = 56 * 1024 * 1024

BRANCH_WIDTH = 512
HEAD_DIM = 128
N_HEADS = BRANCH_WIDTH // HEAD_DIM
DIFF_QK_DIM = 64
GDN_CONV = 4
SC_CONV = 3
NORM_EPS = 1e-6
L2_EPS = 1e-6
NEG_BIG = -1e30
LOG2E = math.log2(math.e)
INV_BASE = 16
ONES_ROWS = PACKED_SUBLANES
FINAL_HALO = PACKED_SUBLANES

PROJ_TILE = 512
GDN_CHUNK = 128
ATTN_Q_TILE = 1024
ATTN_K_TILE = 1024
FINAL_TILE = 512


def _dot(a, b):
    return jnp.dot(a.astype(BF16), b.astype(BF16), preferred_element_type=F32)


def _dot_nt(a, b):
    return lax.dot_general(a.astype(BF16), b.astype(BF16), (((1,), (1,)), ((), ())),
                           preferred_element_type=F32)


def _rms(x, w):
    return x * lax.rsqrt(jnp.mean(x * x, axis=-1, keepdims=True) + NORM_EPS) * w


def _sigmoid(x):
    return 1.0 / (1.0 + jnp.exp(-x))


def _silu(x):
    return x * _sigmoid(x)


def _softplus(x):
    return jnp.maximum(x, 0.0) + jnp.log(1.0 + jnp.exp(-jnp.abs(x)))


def _layer_spec(arr, layer, **kwargs):
    zeros = (0,) * (arr.ndim - 1)
    return pl.BlockSpec((None,) + arr.shape[1:], lambda *_: (layer,) + zeros, **kwargs)


def _proj_kernel(x_ref, nw_ref, w_ref, wba_ref, p_ref, ba_ref, *, col_chunk):
    h = _rms(x_ref[...], nw_ref[...])
    hb = h.astype(BF16)
    for c in range(w_ref.shape[1] // col_chunk):
        cols = slice(c * col_chunk, (c + 1) * col_chunk)
        p_ref[:, cols] = jnp.dot(hb, w_ref[:, cols], preferred_element_type=F32).astype(BF16)
    h_rem = (h - hb.astype(F32)).astype(BF16)
    lead = jnp.dot(hb, wba_ref[...], preferred_element_type=F32)
    ba_ref[...] = (lead[:, :LANES] + lead[:, LANES:]
                   + jnp.dot(h_rem, wba_ref[:, :LANES], preferred_element_type=F32))


def _proj_call(x, norm_w, w1, wba, layer, tile):
    n_tok, d = x.shape
    n_out = w1.shape[2]
    return pl.pallas_call(
        functools.partial(_proj_kernel, col_chunk=512),
        grid=(n_tok // tile,),
        in_specs=[pl.BlockSpec((tile, d), lambda i: (i, 0)),
                  _layer_spec(norm_w, layer), _layer_spec(w1, layer), _layer_spec(wba, layer)],
        out_specs=[pl.BlockSpec((tile, n_out), lambda i: (i, 0)),
                   pl.BlockSpec((tile, LANES), lambda i: (i, 0))],
        out_shape=[jax.ShapeDtypeStruct((n_tok, n_out), BF16),
                   jax.ShapeDtypeStruct((n_tok, LANES), F32)],
        compiler_params=pltpu.CompilerParams(dimension_semantics=("arbitrary",),
                                             vmem_limit_bytes=VMEM_LIMIT),
        name="proj",
    )(x, norm_w, w1, wba)


def _unit_lower_inverses(lms, xr):
    n = lms[0].shape[0]
    eye = jnp.where(xr == 0, 1.0, 0.0).astype(F32)
    base = xr < INV_BASE
    powers = [jnp.where(base, -lm, 0.0) for lm in lms]
    ts = [eye + p for p in powers]
    step = 2
    while step < INV_BASE:
        powers = [_dot(p, p) for p in powers]
        ts = [t + _dot(t, p) for t, p in zip(ts, powers)]
        step *= 2
    size = INV_BASE
    while size < n:
        band = (xr >= size) & (xr < 2 * size)
        xs = [_dot(jnp.where(band, lm, 0.0), t) for lm, t in zip(lms, ts)]
        ts = [t - _dot(t, x) for t, x in zip(ts, xs)]
        size *= 2
    return ts


def _gdn_kernel(p_ref, ba_ref, cw_ref, alog_ref, dtb_ref, nw_ref, o_ref, xbuf, state, *, chunk):
    t_idx = pl.program_id(0)
    halo = SUBLANES
    n_batch = p_ref.shape[0]
    chains = [(b, h) for b in range(n_batch) for h in range(N_HEADS)]

    @pl.when(t_idx == 0)
    def _():
        xbuf[:, 0:halo, :] = jnp.zeros((n_batch, halo, xbuf.shape[2]), F32)
        state[...] = jnp.zeros(state.shape, F32)

    xbuf[:, halo:, :] = p_ref[...].astype(F32)

    ri = lax.broadcasted_iota(jnp.int32, (chunk, chunk), 0)
    ci = lax.broadcasted_iota(jnp.int32, (chunk, chunk), 1)
    tril = ri >= ci
    strict = ri > ci
    xr = ri ^ ci
    tril_ones = jnp.where(tril, 1.0, 0.0).astype(F32)

    gbs, gcs, gc_ts = [], [], []
    for b in range(n_batch):
        pre = ba_ref[b]
        lane = lax.broadcasted_iota(jnp.int32, pre.shape, 1)
        g_all = -jnp.exp(alog_ref[...]) * _softplus(pre + dtb_ref[...])
        gb = jnp.where(lane < N_HEADS, _sigmoid(pre), jnp.where(lane < 2 * N_HEADS, g_all, 0.0))
        gc = jnp.dot(tril_ones, gb, preferred_element_type=F32, precision=lax.Precision.HIGHEST)
        gbs.append(gb)
        gcs.append(gc)
        gc_ts.append(gc.T)

    def conv_silu(b, col):
        acc = (cw_ref[GDN_CONV - 1:GDN_CONV, col:col + HEAD_DIM]
               * xbuf[b, halo:halo + chunk, col:col + HEAD_DIM])
        for tap in range(GDN_CONV - 1):
            shift = GDN_CONV - 1 - tap
            acc = acc + (cw_ref[tap:tap + 1, col:col + HEAD_DIM]
                         * xbuf[b, halo - shift:halo - shift + chunk, col:col + HEAD_DIM])
        return _silu(acc)

    def l2n(v):
        return v * lax.rsqrt(jnp.sum(v * v, axis=-1, keepdims=True) + L2_EPS)

    qs = [l2n(conv_silu(b, h * HEAD_DIM)) * (HEAD_DIM ** -0.5) for b, h in chains]
    ks = [l2n(conv_silu(b, BRANCH_WIDTH + h * HEAD_DIM)) for b, h in chains]
    vs = [conv_silu(b, 2 * BRANCH_WIDTH + h * HEAD_DIM) for b, h in chains]
    betas = [gbs[b][:, h:h + 1] for b, h in chains]
    gcols = [gcs[b][:, N_HEADS + h:N_HEADS + h + 1] for b, h in chains]
    grows = [gc_ts[b][N_HEADS + h:N_HEADS + h + 1, :] for b, h in chains]
    glasts = [gcs[b][chunk - 1:chunk, N_HEADS + h:N_HEADS + h + 1] for b, h in chains]
    decays = [jnp.exp(jnp.where(tril, gc - gr, NEG_BIG)) for gc, gr in zip(gcols, grows)]
    kbs = [k * beta for k, beta in zip(ks, betas)]
    lms = [jnp.where(strict, _dot_nt(kb, k) * d, 0.0) for kb, k, d in zip(kbs, ks, decays)]
    t_invs = _unit_lower_inverses(lms, xr)
    egs = [jnp.exp(gc) for gc in gcols]
    sols = [_dot(t, jnp.concatenate([v * beta, kb * eg], axis=1))
            for t, v, beta, kb, eg in zip(t_invs, vs, betas, kbs, egs)]
    qks = [_dot_nt(q, k) * d for q, k, d in zip(qs, ks, decays)]
    s_olds = [state[c] for c in range(len(chains))]
    wss = [_dot(jnp.concatenate([sol[:, HEAD_DIM:], q * eg], axis=0), s_old)
           for sol, q, eg, s_old in zip(sols, qs, egs, s_olds)]
    v_news = [sol[:, :HEAD_DIM] - ws[:chunk] for sol, ws in zip(sols, wss)]
    outs = [ws[chunk:] + _dot(qk, v_new) for ws, qk, v_new in zip(wss, qks, v_news)]
    for c, (b, h) in enumerate(chains):
        k_dec = ks[c] * jnp.exp(glasts[c] - gcols[c])
        state[c] = s_olds[c] * jnp.exp(glasts[c]) + _dot(k_dec.T, v_news[c])
        o_ref[b, :, h * HEAD_DIM:(h + 1) * HEAD_DIM] = _rms(outs[c], nw_ref[...]).astype(BF16)

    xbuf[:, 0:halo, :] = xbuf[:, chunk:chunk + halo, :]


def _gdn_call(p1, ba, conv_w, alog_rows, dtb_rows, norm_w, layer, batch, chunk):
    n_tok = p1.shape[0]
    seq = n_tok // batch
    width = 3 * BRANCH_WIDTH
    out = pl.pallas_call(
        functools.partial(_gdn_kernel, chunk=chunk),
        grid=(seq // chunk,),
        in_specs=[pl.BlockSpec((batch, chunk, width), lambda t: (0, t, 0)),
                  pl.BlockSpec((batch, chunk, LANES), lambda t: (0, t, 0)),
                  _layer_spec(conv_w, layer), _layer_spec(alog_rows, layer),
                  _layer_spec(dtb_rows, layer), _layer_spec(norm_w, layer)],
        out_specs=pl.BlockSpec((batch, chunk, BRANCH_WIDTH), lambda t: (0, t, 0)),
        out_shape=jax.ShapeDtypeStruct((batch, seq, BRANCH_WIDTH), BF16),
        scratch_shapes=[pltpu.VMEM((batch, chunk + SUBLANES, width), F32),
                        pltpu.VMEM((batch * N_HEADS, HEAD_DIM, HEAD_DIM), F32)],
        compiler_params=pltpu.CompilerParams(dimension_semantics=("arbitrary",),
                                             vmem_limit_bytes=VMEM_LIMIT),
        name="gdn",
    )(p1.reshape(batch, seq, p1.shape[1]), ba.reshape(batch, seq, LANES), conv_w, alog_rows,
      dtb_rows, norm_w)
    return out.reshape(n_tok, BRANCH_WIDTH)


def _attn_kernel(q_ref, k_ref, v_ref, lam_ref, li_ref, nw_ref, o_ref,
                 vt_scr, kpos_scr, s_scr, smax_scr, m_scr, acc_scr, *, tq, tk):
    head = pl.program_id(1)
    qi = pl.program_id(2)
    seq = k_ref.shape[0]
    r = 2 * tq
    slope = jnp.float32(LOG2E * 2.0 ** (-8.0 / N_HEADS))
    for hh in range(1, N_HEADS):
        slope = jnp.where(head == hh, jnp.float32(LOG2E * 2.0 ** (-8.0 * (hh + 1) / N_HEADS)), slope)

    @pl.when(qi == 0)
    def _():
        def transpose_block(blk, carry):
            st = pl.multiple_of(blk * LANES, LANES)
            vt_scr[0:HEAD_DIM, pl.ds(st, LANES)] = v_ref[pl.ds(st, LANES), :].astype(F32).T.astype(BF16)
            return carry
        lax.fori_loop(0, seq // LANES, transpose_block, 0)
        vt_scr[HEAD_DIM:, :] = jnp.ones((vt_scr.shape[0] - HEAD_DIM, seq), BF16)
        key = lax.broadcasted_iota(jnp.int32, (tk, LANES), 0)
        ln = lax.broadcasted_iota(jnp.int32, (tk, LANES), 1)
        kpos = jnp.where(ln < 3, key // 16, jnp.where(ln < 6, key % 16, 0))
        kpos_scr[...] = kpos.astype(F32).astype(BF16)

    q = q_ref[...].astype(F32) * (LOG2E * DIFF_QK_DIM ** -0.5)
    lane = lax.broadcasted_iota(jnp.int32, q.shape, 1)
    qq = jnp.concatenate([jnp.where(lane < DIFF_QK_DIM, q, 0.0),
                          jnp.where(lane >= DIFF_QK_DIM, q, 0.0)], axis=0).astype(BF16)
    sv = jnp.full((1, LANES), slope, F32)
    s_hi = sv.astype(BF16).astype(F32)
    s_mid = (sv - s_hi).astype(BF16).astype(F32)
    s_lo = (sv - s_hi - s_mid).astype(BF16).astype(F32)
    ln1 = lax.broadcasted_iota(jnp.int32, (SUBLANES, LANES), 1)
    pieces = (s_hi * 16.0, s_mid * 16.0, s_lo * 16.0, s_hi, s_mid, s_lo)
    srow = jnp.zeros((SUBLANES, LANES), F32)
    for idx, piece in enumerate(pieces):
        srow = jnp.where(ln1 == idx, piece, srow)
    srow = jnp.broadcast_to(srow[0:1], (r, LANES)).astype(BF16)
    qq = jnp.concatenate([qq, srow], axis=1)

    m_scr[...] = jnp.full(m_scr.shape, NEG_BIG, F32)
    acc_scr[...] = jnp.zeros(acc_scr.shape, F32)

    def scores(j, slot, masked):
        start = pl.multiple_of(j * tk, tk)
        kt = jnp.concatenate([k_ref[pl.ds(start, tk), :], kpos_scr[...]], axis=1)
        s = lax.dot_general(kt, qq, (((1,), (1,)), ((), ())), preferred_element_type=F32)
        if masked:
            key = lax.broadcasted_iota(jnp.int32, (tk, r), 0)
            qry = lax.broadcasted_iota(jnp.int32, (tk, r), 1)
            qry = jnp.where(qry >= tq, qry - tq, qry)
            s = jnp.where(key + (j * tk - qi * tq) <= qry, s, NEG_BIG)
        s_scr[slot] = s
        smax_scr[slot] = jnp.max(s, axis=0, keepdims=True)

    def accumulate(j, slot):
        start = pl.multiple_of(j * tk, tk)
        vt = vt_scr[:, pl.ds(start, tk)]
        cj = slope * (j * tk - qi * tq).astype(F32)
        m_prev = m_scr[...]
        m_new = jnp.maximum(m_prev, smax_scr[slot] + cj)
        alpha = jnp.exp2(m_prev - m_new)
        p = jnp.exp2((s_scr[slot] - (m_new - cj)).astype(BF16))
        acc_scr[...] = alpha * acc_scr[...] + jnp.dot(vt, p, preferred_element_type=F32)
        m_scr[...] = m_new

    n_full = (qi * tq) // tk
    scores(n_full, 0, True)

    def body(t, carry):
        j = jnp.where(t == 0, n_full, t - 1)
        j_next = jnp.maximum(jnp.minimum(t, n_full - 1), 0)
        for slot in range(2):
            @pl.when(t % 2 == slot)
            def _():
                scores(j_next, 1 - slot, False)
                accumulate(j, slot)
        return carry

    lax.fori_loop(0, n_full + 1, body, 0)

    lv = lam_ref[...]
    lam_init = li_ref[:, 0:1]
    lam = (jnp.exp(jnp.sum(lv[0:1] * lv[1:2], axis=-1, keepdims=True))
           - jnp.exp(jnp.sum(lv[2:3] * lv[3:4], axis=-1, keepdims=True)) + lam_init)
    acc = acc_scr[...]
    o_all = acc[:HEAD_DIM] / acc[HEAD_DIM:HEAD_DIM + 1]
    o = (o_all[:, :tq] - lam * o_all[:, tq:]).T
    o_ref[...] = (_rms(o, nw_ref[...]) * (1.0 - lam_init)).astype(BF16)


def _attn_call(p1, lam_vecs, lam_init_rows, norm_w, layer, batch, tq, tk):
    n_tok = p1.shape[0]
    seq = n_tok // batch
    n_q = seq // tq
    assert tk % tq == 0 and seq % tk == 0
    qcol = 3 * BRANCH_WIDTH // HEAD_DIM
    kcol = qcol + N_HEADS
    vcol = kcol + N_HEADS
    return pl.pallas_call(
        functools.partial(_attn_kernel, tq=tq, tk=tk),
        grid=(batch, N_HEADS, n_q),
        in_specs=[pl.BlockSpec((tq, HEAD_DIM), lambda b, h, i: (b * n_q + i, qcol + h)),
                  pl.BlockSpec((seq, HEAD_DIM), lambda b, h, i: (b, kcol + h)),
                  pl.BlockSpec((seq, HEAD_DIM), lambda b, h, i: (b, vcol + h)),
                  _layer_spec(lam_vecs, layer), _layer_spec(lam_init_rows, layer),
                  _layer_spec(norm_w, layer)],
        out_specs=pl.BlockSpec((tq, HEAD_DIM), lambda b, h, i: (b * n_q + i, h)),
        out_shape=jax.ShapeDtypeStruct((n_tok, BRANCH_WIDTH), BF16),
        scratch_shapes=[pltpu.VMEM((HEAD_DIM + ONES_ROWS, seq), BF16),
                        pltpu.VMEM((tk, LANES), BF16),
                        pltpu.VMEM((2, tk, 2 * tq), F32),
                        pltpu.VMEM((2, 1, 2 * tq), F32),
                        pltpu.VMEM((1, 2 * tq), F32),
                        pltpu.VMEM((HEAD_DIM + ONES_ROWS, 2 * tq), F32)],
        compiler_params=pltpu.CompilerParams(
            dimension_semantics=("arbitrary", "arbitrary", "arbitrary"),
            vmem_limit_bytes=VMEM_LIMIT),
        name="diff_attn",
    )(p1, p1, p1, lam_vecs, lam_init_rows, norm_w)


def _final_kernel(x_ref, xh_ref, og_ref, oa_ref, npre_ref, npost_ref, w2_ref, scw_ref, wb_ref,
                  wo_ref, out_ref, *, tiles_per_seq):
    i = pl.program_id(0)
    w = BRANCH_WIDTH
    x = x_ref[...]
    tile, d = x.shape
    hb = _rms(x, npre_ref[...]).astype(BF16)
    hh = _rms(xh_ref[...], npre_ref[...]).astype(BF16)

    def proj(cols):
        return jnp.dot(hb, w2_ref[:, cols], preferred_element_type=F32)

    cxe = jnp.dot(jnp.concatenate([hb, hh], axis=0), w2_ref[:, w:3 * w], preferred_element_type=F32)
    cxe = cxe[:, :w] * cxe[:, w:]
    cx = cxe[:tile]
    cxh = jnp.where(i % tiles_per_seq == 0, 0.0, cxe[tile:])
    rows = lax.broadcasted_iota(jnp.int32, (tile, w), 0)
    conv = scw_ref[SC_CONV - 1:SC_CONV, :] * cx
    for tap in range(SC_CONV - 1):
        shift = SC_CONV - 1 - tap
        shifted = pltpu.roll(cx, shift, axis=0)
        for r in range(shift):
            src = FINAL_HALO - shift + r
            shifted = jnp.where(rows == r, cxh[src:src + 1, :], shifted)
        conv = conv + scw_ref[tap:tap + 1, :] * shifted
    y_c = proj(slice(0, w)) * conv * _silu(proj(slice(5 * w, 6 * w)))
    y_a = og_ref[...].astype(F32) * _silu(proj(slice(3 * w, 4 * w)))
    y_b = oa_ref[...].astype(F32) * _silu(proj(slice(4 * w, 5 * w)))

    merged = jnp.zeros((tile, d), F32)
    for k, y in enumerate((y_a, y_b, y_c)):
        gate = _sigmoid(proj(slice(6 * w + k * d, 6 * w + (k + 1) * d)))
        merged = merged + gate * jnp.dot(y.astype(BF16), wb_ref[k], preferred_element_type=F32)
    out = jnp.dot(merged.astype(BF16), wo_ref[...], preferred_element_type=F32)
    out_ref[...] = x + _rms(out, npost_ref[...])


def _final_call(x, o_gdn, o_attn, norm_pre, norm_post, w2, sc_w, w_branch, w_out, layer, batch, tile):
    n_tok, d = x.shape
    tiles_per_seq = n_tok // batch // tile
    halo_blocks = tile // FINAL_HALO
    const = dict(pipeline_mode=pl.Buffered(1))
    return pl.pallas_call(
        functools.partial(_final_kernel, tiles_per_seq=tiles_per_seq),
        grid=(n_tok // tile,),
        in_specs=[pl.BlockSpec((tile, d), lambda i: (i, 0)),
                  pl.BlockSpec((FINAL_HALO, d), lambda i: (jnp.maximum(i * halo_blocks - 1, 0), 0)),
                  pl.BlockSpec((tile, BRANCH_WIDTH), lambda i: (i, 0)),
                  pl.BlockSpec((tile, BRANCH_WIDTH), lambda i: (i, 0)),
                  _layer_spec(norm_pre, layer), _layer_spec(norm_post, layer),
                  _layer_spec(w2, layer, **const), _layer_spec(sc_w, layer),
                  _layer_spec(w_branch, layer, **const), _layer_spec(w_out, layer, **const)],
        out_specs=pl.BlockSpec((tile, d), lambda i: (i, 0)),
        out_shape=jax.ShapeDtypeStruct((n_tok, d), F32),
        compiler_params=pltpu.CompilerParams(dimension_semantics=("arbitrary",),
                                             vmem_limit_bytes=VMEM_LIMIT),
        name="merge_out",
    )(x, x, o_gdn, o_attn, norm_pre, norm_post, w2, sc_w, w_branch, w_out)


def _regroup_w_in(w_in):
    w = BRANCH_WIDTH
    o = 0
    parts = {}
    for name, size in (("gdn_qkv", 3 * w), ("gdn_z", w), ("gdn_b", N_HEADS), ("gdn_a", N_HEADS),
                       ("d_q", w), ("d_k", w), ("d_v", w), ("d_z", w),
                       ("s_b", w), ("s_c", w), ("s_x", w), ("s_z", w), ("merge", None)):
        size = w_in.shape[2] - o if size is None else size
        parts[name] = w_in[:, :, o:o + size]
        o += size
    w1 = jnp.concatenate([parts[n] for n in ("gdn_qkv", "d_q", "d_k", "d_v")], axis=2).astype(BF16)
    wba = jnp.concatenate([parts["gdn_b"], parts["gdn_a"]], axis=2)
    wba = jnp.pad(wba, ((0, 0), (0, 0), (0, LANES - wba.shape[2])))
    wba_lead = wba.astype(BF16)
    wba_rem = (wba - wba_lead.astype(F32)).astype(BF16)
    w2 = jnp.concatenate([parts[n] for n in ("s_b", "s_c", "s_x", "gdn_z", "d_z", "s_z", "merge")],
                         axis=2).astype(BF16)
    return w1, jnp.concatenate([wba_lead, wba_rem], axis=2), w2


def _head_rows(vals):
    return jnp.pad(vals.astype(F32), ((0, 0), (N_HEADS, LANES - 2 * N_HEADS)))[:, None, :]


def kernel(x, norm_pre, norm_post, w_in, gdn_conv_w, gdn_a_log, gdn_dt_bias, gdn_norm_w,
           diff_lambda, diff_norm_w, sc_conv_w, w_branch, w_out):
    bsz, seq, d = x.shape
    depth = w_in.shape[0]
    w1, wba, w2 = _regroup_w_in(w_in)
    wb = w_branch.astype(BF16)
    wo = w_out.astype(BF16)
    npre = norm_pre[:, None, :]
    npost = norm_post[:, None, :]
    alog_rows = _head_rows(gdn_a_log)
    dtb_rows = _head_rows(gdn_dt_bias)
    gdn_nw = gdn_norm_w[:, None, :]
    diff_nw = diff_norm_w[:, None, :]
    lam_init_rows = jnp.asarray(np.broadcast_to(
        np.array([0.8 - 0.6 * math.exp(-0.3 * l) for l in range(depth)], np.float32)[:, None, None],
        (depth, 1, LANES)))
    xf = x.reshape(bsz * seq, d)
    for l in range(depth):
        p1, ba = _proj_call(xf, npre, w1, wba, l, min(PROJ_TILE, seq))
        o_gdn = _gdn_call(p1, ba, gdn_conv_w, alog_rows, dtb_rows, gdn_nw, l, bsz,
                          min(GDN_CHUNK, seq))
        o_attn = _attn_call(p1, diff_lambda, lam_init_rows, diff_nw, l, bsz,
                            min(ATTN_Q_TILE, seq), min(ATTN_K_TILE, seq))
        xf = _final_call(xf, o_gdn, o_attn, npre, npost, w2, sc_conv_w, wb, wo, l, bsz,
                         min(FINAL_TILE, seq))
    return xf.reshape(bsz, seq, d)
```

```python
import functools
import math

import numpy as np
import jax
import jax.numpy as jnp
from jax import lax
from jax.experimental import pallas as pl
from jax.experimental.pallas import tpu as pltpu

F32 = jnp.float32
BF16 = jnp.bfloat16

LANES = 128
SUBLANES = 8
PACKED_SUBLANES = 16
VMEM_LIMIT = 56 * 1024 * 1024

BRANCH_WIDTH = 512
HEAD_DIM = 128
N_HEADS = BRANCH_WIDTH // HEAD_DIM
DIFF_QK_DIM = 64
GDN_CONV = 4
SC_CONV = 3
NORM_EPS = 1e-6
L2_EPS = 1e-6
NEG_BIG = -1e30
LOG2E = math.log2(math.e)
INV_BASE = 16
ONES_ROWS = PACKED_SUBLANES
FINAL_HALO = PACKED_SUBLANES

PROJ_TILE = 512
GDN_CHUNK = 128
ATTN_TILE = 1024
ATTN_COL_BLOCK = 256
FINAL_TILE = 512


def _dot(a, b):
    return jnp.dot(a.astype(BF16), b.astype(BF16), preferred_element_type=F32)


def _dot_nt(a, b):
    return lax.dot_general(a.astype(BF16), b.astype(BF16), (((1,), (1,)), ((), ())),
                           preferred_element_type=F32)


def _rms(x, w):
    return x * lax.rsqrt(jnp.mean(x * x, axis=-1, keepdims=True) + NORM_EPS) * w


def _sigmoid(x):
    return 0.5 * jnp.tanh(0.5 * x) + 0.5


def _silu(x):
    return x * _sigmoid(x)


def _softplus(x):
    return jnp.maximum(x, 0.0) + jnp.log(1.0 + jnp.exp(-jnp.abs(x)))


def _layer_spec(arr, layer, **kwargs):
    zeros = (0,) * (arr.ndim - 1)
    return pl.BlockSpec((None,) + arr.shape[1:], lambda *_: (layer,) + zeros, **kwargs)


def _proj_kernel(x_ref, nw_ref, w_ref, wba_ref, p_ref, ba_ref, *, col_chunk):
    h = _rms(x_ref[...], nw_ref[...])
    hb = h.astype(BF16)
    for c in range(w_ref.shape[1] // col_chunk):
        cols = slice(c * col_chunk, (c + 1) * col_chunk)
        p_ref[:, cols] = jnp.dot(hb, w_ref[:, cols], preferred_element_type=F32).astype(BF16)
    h_rem = (h - hb.astype(F32)).astype(BF16)
    lead = jnp.dot(hb, wba_ref[...], preferred_element_type=F32)
    ba_ref[...] = (lead[:, :LANES] + lead[:, LANES:]
                   + jnp.dot(h_rem, wba_ref[:, :LANES], preferred_element_type=F32))


def _proj_call(x, norm_w, w1, wba, layer, tile):
    n_tok, d = x.shape
    n_out = w1.shape[2]
    return pl.pallas_call(
        functools.partial(_proj_kernel, col_chunk=512),
        grid=(n_tok // tile,),
        in_specs=[pl.BlockSpec((tile, d), lambda i: (i, 0)),
                  _layer_spec(norm_w, layer), _layer_spec(w1, layer), _layer_spec(wba, layer)],
        out_specs=[pl.BlockSpec((tile, n_out), lambda i: (i, 0)),
                   pl.BlockSpec((tile, LANES), lambda i: (i, 0))],
        out_shape=[jax.ShapeDtypeStruct((n_tok, n_out), BF16),
                   jax.ShapeDtypeStruct((n_tok, LANES), F32)],
        compiler_params=pltpu.CompilerParams(dimension_semantics=("arbitrary",),
                                             vmem_limit_bytes=VMEM_LIMIT),
        name="proj",
    )(x, norm_w, w1, wba)


def _unit_lower_inverses(lms, xr):
    n = lms[0].shape[0]
    eye = jnp.where(xr == 0, 1.0, 0.0).astype(F32)
    base = xr < INV_BASE
    powers = [jnp.where(base, -lm, 0.0) for lm in lms]
    ts = [eye + p for p in powers]
    step = 2
    while step < INV_BASE:
        powers = [_dot(p, p) for p in powers]
        ts = [t + _dot(t, p) for t, p in zip(ts, powers)]
        step *= 2
    size = INV_BASE
    while size < n:
        band = (xr >= size) & (xr < 2 * size)
        xs = [_dot(jnp.where(band, lm, 0.0), t) for lm, t in zip(lms, ts)]
        ts = [t - _dot(t, x) for t, x in zip(ts, xs)]
        size *= 2
    return ts


def _gdn_kernel(p_ref, ba_ref, cw_ref, alog_ref, dtb_ref, nw_ref, o_ref, xbuf, state, *, chunk):
    t_idx = pl.program_id(0)
    halo = SUBLANES
    n_batch = p_ref.shape[0]
    chains = [(b, h) for b in range(n_batch) for h in range(N_HEADS)]

    @pl.when(t_idx == 0)
    def _():
        xbuf[:, 0:halo, :] = jnp.zeros((n_batch, halo, xbuf.shape[2]), F32)
        state[...] = jnp.zeros(state.shape, F32)

    xbuf[:, halo:, :] = p_ref[...].astype(F32)

    ri = lax.broadcasted_iota(jnp.int32, (chunk, chunk), 0)
    ci = lax.broadcasted_iota(jnp.int32, (chunk, chunk), 1)
    tril = ri >= ci
    strict = ri > ci
    xr = ri ^ ci
    tril_ones = jnp.where(tril, 1.0, 0.0).astype(F32)

    gbs, gcs, gc_ts = [], [], []
    for b in range(n_batch):
        pre = ba_ref[b]
        lane = lax.broadcasted_iota(jnp.int32, pre.shape, 1)
        g_all = -jnp.exp(alog_ref[...]) * _softplus(pre + dtb_ref[...])
        gb = jnp.where(lane < N_HEADS, _sigmoid(pre), jnp.where(lane < 2 * N_HEADS, g_all, 0.0))
        gc = jnp.dot(tril_ones, gb, preferred_element_type=F32, precision=lax.Precision.HIGHEST)
        gbs.append(gb)
        gcs.append(gc)
        gc_ts.append(gc.T)

    def conv_silu(b, col):
        acc = (cw_ref[GDN_CONV - 1:GDN_CONV, col:col + HEAD_DIM]
               * xbuf[b, halo:halo + chunk, col:col + HEAD_DIM])
        for tap in range(GDN_CONV - 1):
            shift = GDN_CONV - 1 - tap
            acc = acc + (cw_ref[tap:tap + 1, col:col + HEAD_DIM]
                         * xbuf[b, halo - shift:halo - shift + chunk, col:col + HEAD_DIM])
        return _silu(acc)

    def l2n(v):
        return v * lax.rsqrt(jnp.sum(v * v, axis=-1, keepdims=True) + L2_EPS)

    qs = [l2n(conv_silu(b, h * HEAD_DIM)) * (HEAD_DIM ** -0.5) for b, h in chains]
    ks = [l2n(conv_silu(b, BRANCH_WIDTH + h * HEAD_DIM)) for b, h in chains]
    vs = [conv_silu(b, 2 * BRANCH_WIDTH + h * HEAD_DIM) for b, h in chains]
    betas = [gbs[b][:, h:h + 1] for b, h in chains]
    gcols = [gcs[b][:, N_HEADS + h:N_HEADS + h + 1] for b, h in chains]
    grows = [gc_ts[b][N_HEADS + h:N_HEADS + h + 1, :] for b, h in chains]
    glasts = [gcs[b][chunk - 1:chunk, N_HEADS + h:N_HEADS + h + 1] for b, h in chains]
    decays = [jnp.exp(jnp.where(tril, gc - gr, NEG_BIG)) for gc, gr in zip(gcols, grows)]
    kbs = [k * beta for k, beta in zip(ks, betas)]
    lms = [jnp.where(strict, _dot_nt(kb, k) * d, 0.0) for kb, k, d in zip(kbs, ks, decays)]
    t_invs = _unit_lower_inverses(lms, xr)
    egs = [jnp.exp(gc) for gc in gcols]
    sols = [_dot(t, jnp.concatenate([v * beta, kb * eg], axis=1))
            for t, v, beta, kb, eg in zip(t_invs, vs, betas, kbs, egs)]
    qks = [_dot_nt(q, k) * d for q, k, d in zip(qs, ks, decays)]
    s_olds = [state[c] for c in range(len(chains))]
    wss = [_dot(jnp.concatenate([sol[:, HEAD_DIM:], q * eg], axis=0), s_old)
           for sol, q, eg, s_old in zip(sols, qs, egs, s_olds)]
    v_news = [sol[:, :HEAD_DIM] - ws[:chunk] for sol, ws in zip(sols, wss)]
    outs = [ws[chunk:] + _dot(qk, v_new) for ws, qk, v_new in zip(wss, qks, v_news)]
    for c, (b, h) in enumerate(chains):
        k_dec = ks[c] * jnp.exp(glasts[c] - gcols[c])
        state[c] = s_olds[c] * jnp.exp(glasts[c]) + _dot(k_dec.T, v_news[c])
        o_ref[b, :, h * HEAD_DIM:(h + 1) * HEAD_DIM] = _rms(outs[c], nw_ref[...]).astype(BF16)

    xbuf[:, 0:halo, :] = xbuf[:, chunk:chunk + halo, :]


def _gdn_call(p1, ba, conv_w, alog_rows, dtb_rows, norm_w, layer, batch, chunk):
    n_tok = p1.shape[0]
    seq = n_tok // batch
    width = 3 * BRANCH_WIDTH
    out = pl.pallas_call(
        functools.partial(_gdn_kernel, chunk=chunk),
        grid=(seq // chunk,),
        in_specs=[pl.BlockSpec((batch, chunk, width), lambda t: (0, t, 0)),
                  pl.BlockSpec((batch, chunk, LANES), lambda t: (0, t, 0)),
                  _layer_spec(conv_w, layer), _layer_spec(alog_rows, layer),
                  _layer_spec(dtb_rows, layer), _layer_spec(norm_w, layer)],
        out_specs=pl.BlockSpec((batch, chunk, BRANCH_WIDTH), lambda t: (0, t, 0)),
        out_shape=jax.ShapeDtypeStruct((batch, seq, BRANCH_WIDTH), BF16),
        scratch_shapes=[pltpu.VMEM((batch, chunk + SUBLANES, width), F32),
                        pltpu.VMEM((batch * N_HEADS, HEAD_DIM, HEAD_DIM), F32)],
        compiler_params=pltpu.CompilerParams(dimension_semantics=("arbitrary",),
                                             vmem_limit_bytes=VMEM_LIMIT),
        name="gdn",
    )(p1.reshape(batch, seq, p1.shape[1]), ba.reshape(batch, seq, LANES), conv_w, alog_rows,
      dtb_rows, norm_w)
    return out.reshape(n_tok, BRANCH_WIDTH)


def _attn_kernel(q_ref, qn_ref, k_ref, v_ref, lam_ref, li_ref, nw_ref, o_ref,
                 vt_scr, kpos_scr, qq_scr, qqn_scr, s_scr, smax_scr, m_scr, acc_scr, *, tile):
    head = pl.program_id(1)
    qi = pl.program_id(2)
    seq = k_ref.shape[0]
    r = 2 * tile
    slope = jnp.float32(LOG2E * 2.0 ** (-8.0 / N_HEADS))
    for hh in range(1, N_HEADS):
        slope = jnp.where(head == hh, jnp.float32(LOG2E * 2.0 ** (-8.0 * (hh + 1) / N_HEADS)), slope)

    @pl.when(qi == 0)
    def _():
        def transpose_block(blk, carry):
            st = pl.multiple_of(blk * LANES, LANES)
            vt_scr[0:HEAD_DIM, pl.ds(st, LANES)] = v_ref[pl.ds(st, LANES), :].astype(F32).T.astype(BF16)
            return carry
        lax.fori_loop(0, seq // LANES, transpose_block, 0)
        vt_scr[HEAD_DIM:, :] = jnp.ones((vt_scr.shape[0] - HEAD_DIM, seq), BF16)
        key = lax.broadcasted_iota(jnp.int32, (tile, LANES), 0)
        ln = lax.broadcasted_iota(jnp.int32, (tile, LANES), 1)
        kpos = jnp.where(ln < 3, key // 16, jnp.where(ln < 6, key % 16, 0))
        kpos_scr[...] = kpos.astype(F32).astype(BF16)

    sv = jnp.full((1, LANES), slope, F32)
    s_hi = sv.astype(BF16).astype(F32)
    s_mid = (sv - s_hi).astype(BF16).astype(F32)
    s_lo = (sv - s_hi - s_mid).astype(BF16).astype(F32)
    ln1 = lax.broadcasted_iota(jnp.int32, (SUBLANES, LANES), 1)
    pieces = (s_hi * 16.0, s_mid * 16.0, s_lo * 16.0, s_hi, s_mid, s_lo)
    srow = jnp.zeros((SUBLANES, LANES), F32)
    for idx, piece in enumerate(pieces):
        srow = jnp.where(ln1 == idx, piece, srow)
    srow = jnp.broadcast_to(srow[0:1], (r, LANES)).astype(BF16)

    def stacked_queries(ref):
        q = ref[...].astype(F32) * (LOG2E * DIFF_QK_DIM ** -0.5)
        lane = lax.broadcasted_iota(jnp.int32, q.shape, 1)
        qq = jnp.concatenate([jnp.where(lane < DIFF_QK_DIM, q, 0.0),
                              jnp.where(lane >= DIFF_QK_DIM, q, 0.0)], axis=0).astype(BF16)
        return jnp.concatenate([qq, srow], axis=1)

    col_blocks = [slice(n * ATTN_COL_BLOCK, (n + 1) * ATTN_COL_BLOCK)
                  for n in range(r // ATTN_COL_BLOCK)]

    def score_block(qq_ref, key_tile, slot, cols, diagonal):
        start = pl.multiple_of(key_tile * tile, tile)
        n_keys = cols.start % tile + ATTN_COL_BLOCK if diagonal else tile
        kt = jnp.concatenate([k_ref[pl.ds(start, n_keys), :], kpos_scr[0:n_keys, :]], axis=1)
        s = lax.dot_general(kt, qq_ref[cols, :], (((1,), (1,)), ((), ())),
                            preferred_element_type=F32)
        if diagonal:
            key = lax.broadcasted_iota(jnp.int32, s.shape, 0)
            qry = lax.broadcasted_iota(jnp.int32, s.shape, 1) + cols.start % tile
            s = jnp.where(key <= qry, s, NEG_BIG)
            if n_keys < tile:
                s_scr[slot, n_keys:, cols] = jnp.full((tile - n_keys, ATTN_COL_BLOCK), NEG_BIG, F32)
        s_scr[slot, 0:n_keys, cols] = s
        smax_scr[slot, :, cols] = jnp.max(s, axis=0, keepdims=True)

    def accumulate_block(key_tile, slot, cols):
        start = pl.multiple_of(key_tile * tile, tile)
        cj = slope * ((key_tile - qi) * tile).astype(F32)
        m_prev = m_scr[:, cols]
        m_new = jnp.maximum(m_prev, smax_scr[slot, :, cols] + cj)
        alpha = jnp.exp2(m_prev - m_new)
        p = jnp.exp2((s_scr[slot, :, cols] - (m_new - cj)).astype(BF16))
        acc_scr[:, cols] = alpha * acc_scr[:, cols] + jnp.dot(
            vt_scr[:, pl.ds(start, tile)], p, preferred_element_type=F32)
        m_scr[:, cols] = m_new

    qq_scr[...] = stacked_queries(q_ref)

    @pl.when(qi == 0)
    def _():
        for cols in col_blocks:
            score_block(qq_scr, 0, 0, cols, True)

    m_scr[...] = jnp.full(m_scr.shape, NEG_BIG, F32)
    acc_scr[...] = jnp.zeros(acc_scr.shape, F32)

    first_slot = ((qi * (qi + 1)) // 2) % 2

    def body(t, carry):
        key_tile = jnp.where(t == 0, qi, t - 1)
        for slot in range(2):
            @pl.when((first_slot + t) % 2 == slot)
            def _():
                for cols in col_blocks:
                    score_block(qq_scr, t, 1 - slot, cols, False)
                    accumulate_block(key_tile, slot, cols)
        return carry

    lax.fori_loop(0, qi, body, 0)

    last_tile = jnp.where(qi == 0, 0, qi - 1)
    next_q = jnp.minimum(qi + 1, seq // tile - 1)
    for slot in range(2):
        @pl.when((first_slot + qi) % 2 == slot)
        def _():
            qqn_scr[...] = stacked_queries(qn_ref)
            for cols in col_blocks:
                score_block(qqn_scr, next_q, 1 - slot, cols, True)
                accumulate_block(last_tile, slot, cols)

    lv = lam_ref[...]
    lam_init = li_ref[:, 0:1]
    lam = (jnp.exp(jnp.sum(lv[0:1] * lv[1:2], axis=-1, keepdims=True))
           - jnp.exp(jnp.sum(lv[2:3] * lv[3:4], axis=-1, keepdims=True)) + lam_init)
    acc = acc_scr[...]
    o_all = acc[:HEAD_DIM] / acc[HEAD_DIM:HEAD_DIM + 1]
    o = (o_all[:, :tile] - lam * o_all[:, tile:]).T
    o_ref[...] = (_rms(o, nw_ref[...]) * (1.0 - lam_init)).astype(BF16)


def _attn_call(p1, lam_vecs, lam_init_rows, norm_w, layer, batch, tile):
    n_tok = p1.shape[0]
    seq = n_tok // batch
    n_q = seq // tile
    qcol = 3 * BRANCH_WIDTH // HEAD_DIM
    kcol = qcol + N_HEADS
    vcol = kcol + N_HEADS
    return pl.pallas_call(
        functools.partial(_attn_kernel, tile=tile),
        grid=(batch, N_HEADS, n_q),
        in_specs=[pl.BlockSpec((tile, HEAD_DIM), lambda b, h, i: (b * n_q + i, qcol + h)),
                  pl.BlockSpec((tile, HEAD_DIM),
                               lambda b, h, i: (b * n_q + jnp.minimum(i + 1, n_q - 1), qcol + h)),
                  pl.BlockSpec((seq, HEAD_DIM), lambda b, h, i: (b, kcol + h)),
                  pl.BlockSpec((seq, HEAD_DIM), lambda b, h, i: (b, vcol + h)),
                  _layer_spec(lam_vecs, layer), _layer_spec(lam_init_rows, layer),
                  _layer_spec(norm_w, layer)],
        out_specs=pl.BlockSpec((tile, HEAD_DIM), lambda b, h, i: (b * n_q + i, h)),
        out_shape=jax.ShapeDtypeStruct((n_tok, BRANCH_WIDTH), BF16),
        scratch_shapes=[pltpu.VMEM((HEAD_DIM + ONES_ROWS, seq), BF16),
                        pltpu.VMEM((tile, LANES), BF16),
                        pltpu.VMEM((2 * tile, 2 * LANES), BF16),
                        pltpu.VMEM((2 * tile, 2 * LANES), BF16),
                        pltpu.VMEM((2, tile, 2 * tile), F32),
                        pltpu.VMEM((2, 1, 2 * tile), F32),
                        pltpu.VMEM((1, 2 * tile), F32),
                        pltpu.VMEM((HEAD_DIM + ONES_ROWS, 2 * tile), F32)],
        compiler_params=pltpu.CompilerParams(
            dimension_semantics=("arbitrary", "arbitrary", "arbitrary"),
            vmem_limit_bytes=VMEM_LIMIT),
        name="diff_attn",
    )(p1, p1, p1, p1, lam_vecs, lam_init_rows, norm_w)


def _final_kernel(x_ref, xh_ref, og_ref, oa_ref, npre_ref, npost_ref, w2_ref, scw_ref, wb_ref,
                  wo_ref, out_ref, *, tiles_per_seq):
    i = pl.program_id(0)
    w = BRANCH_WIDTH
    x = x_ref[...]
    tile, d = x.shape
    hb = _rms(x, npre_ref[...]).astype(BF16)
    hh = _rms(xh_ref[...], npre_ref[...]).astype(BF16)

    def proj(cols):
        return jnp.dot(hb, w2_ref[:, cols], preferred_element_type=F32)

    cxe = jnp.dot(jnp.concatenate([hb, hh], axis=0), w2_ref[:, w:3 * w], preferred_element_type=F32)
    cxe = cxe[:, :w] * cxe[:, w:]
    cx = cxe[:tile]
    cxh = jnp.where(i % tiles_per_seq == 0, 0.0, cxe[tile:])
    rows = lax.broadcasted_iota(jnp.int32, (tile, w), 0)
    conv = scw_ref[SC_CONV - 1:SC_CONV, :] * cx
    for tap in range(SC_CONV - 1):
        shift = SC_CONV - 1 - tap
        shifted = pltpu.roll(cx, shift, axis=0)
        for r in range(shift):
            src = FINAL_HALO - shift + r
            shifted = jnp.where(rows == r, cxh[src:src + 1, :], shifted)
        conv = conv + scw_ref[tap:tap + 1, :] * shifted
    y_c = proj(slice(0, w)) * conv * _silu(proj(slice(5 * w, 6 * w)))
    y_a = og_ref[...].astype(F32) * _silu(proj(slice(3 * w, 4 * w)))
    y_b = oa_ref[...].astype(F32) * _silu(proj(slice(4 * w, 5 * w)))

    merged = jnp.zeros((tile, d), F32)
    for k, y in enumerate((y_a, y_b, y_c)):
        gate = _sigmoid(proj(slice(6 * w + k * d, 6 * w + (k + 1) * d)))
        merged = merged + gate * jnp.dot(y.astype(BF16), wb_ref[k], preferred_element_type=F32)
    out = jnp.dot(merged.astype(BF16), wo_ref[...], preferred_element_type=F32)
    out_ref[...] = x + _rms(out, npost_ref[...])


def _final_call(x, o_gdn, o_attn, norm_pre, norm_post, w2, sc_w, w_branch, w_out, layer, batch, tile):
    n_tok, d = x.shape
    tiles_per_seq = n_tok // batch // tile
    halo_blocks = tile // FINAL_HALO
    const = dict(pipeline_mode=pl.Buffered(1))
    return pl.pallas_call(
        functools.partial(_final_kernel, tiles_per_seq=tiles_per_seq),
        grid=(n_tok // tile,),
        in_specs=[pl.BlockSpec((tile, d), lambda i: (i, 0)),
                  pl.BlockSpec((FINAL_HALO, d), lambda i: (jnp.maximum(i * halo_blocks - 1, 0), 0)),
                  pl.BlockSpec((tile, BRANCH_WIDTH), lambda i: (i, 0)),
                  pl.BlockSpec((tile, BRANCH_WIDTH), lambda i: (i, 0)),
                  _layer_spec(norm_pre, layer), _layer_spec(norm_post, layer),
                  _layer_spec(w2, layer, **const), _layer_spec(sc_w, layer),
                  _layer_spec(w_branch, layer, **const), _layer_spec(w_out, layer, **const)],
        out_specs=pl.BlockSpec((tile, d), lambda i: (i, 0)),
        out_shape=jax.ShapeDtypeStruct((n_tok, d), F32),
        compiler_params=pltpu.CompilerParams(dimension_semantics=("arbitrary",),
                                             vmem_limit_bytes=VMEM_LIMIT),
        name="merge_out",
    )(x, x, o_gdn, o_attn, norm_pre, norm_post, w2, sc_w, w_branch, w_out)


def _regroup_w_in(w_in):
    w = BRANCH_WIDTH
    o = 0
    parts = {}
    for name, size in (("gdn_qkv", 3 * w), ("gdn_z", w), ("gdn_b", N_HEADS), ("gdn_a", N_HEADS),
                       ("d_q", w), ("d_k", w), ("d_v", w), ("d_z", w),
                       ("s_b", w), ("s_c", w), ("s_x", w), ("s_z", w), ("merge", None)):
        size = w_in.shape[2] - o if size is None else size
        parts[name] = w_in[:, :, o:o + size]
        o += size
    w1 = jnp.concatenate([parts[n] for n in ("gdn_qkv", "d_q", "d_k", "d_v")], axis=2).astype(BF16)
    wba = jnp.concatenate([parts["gdn_b"], parts["gdn_a"]], axis=2)
    wba = jnp.pad(wba, ((0, 0), (0, 0), (0, LANES - wba.shape[2])))
    wba_lead = wba.astype(BF16)
    wba_rem = (wba - wba_lead.astype(F32)).astype(BF16)
    w2 = jnp.concatenate([parts[n] for n in ("s_b", "s_c", "s_x", "gdn_z", "d_z", "s_z", "merge")],
                         axis=2).astype(BF16)
    return w1, jnp.concatenate([wba_lead, wba_rem], axis=2), w2


def _head_rows(vals):
    return jnp.pad(vals.astype(F32), ((0, 0), (N_HEADS, LANES - 2 * N_HEADS)))[:, None, :]


def kernel(x, norm_pre, norm_post, w_in, gdn_conv_w, gdn_a_log, gdn_dt_bias, gdn_norm_w,
           diff_lambda, diff_norm_w, sc_conv_w, w_branch, w_out):
    bsz, seq, d = x.shape
    depth = w_in.shape[0]
    w1, wba, w2 = _regroup_w_in(w_in)
    wb = w_branch.astype(BF16)
    wo = w_out.astype(BF16)
    npre = norm_pre[:, None, :]
    npost = norm_post[:, None, :]
    alog_rows = _head_rows(gdn_a_log)
    dtb_rows = _head_rows(gdn_dt_bias)
    gdn_nw = gdn_norm_w[:, None, :]
    diff_nw = diff_norm_w[:, None, :]
    lam_init_rows = jnp.asarray(np.broadcast_to(
        np.array([0.8 - 0.6 * math.exp(-0.3 * l) for l in range(depth)], np.float32)[:, None, None],
        (depth, 1, LANES)))
    xf = x.reshape(bsz * seq, d)
    for l in range(depth):
        p1, ba = _proj_call(xf, npre, w1, wba, l, min(PROJ_TILE, seq))
        o_gdn = _gdn_call(p1, ba, gdn_conv_w, alog_rows, dtb_rows, gdn_nw, l, bsz,
                          min(GDN_CHUNK, seq))
        o_attn = _attn_call(p1, diff_lambda, lam_init_rows, diff_nw, l, bsz,
                            min(ATTN_TILE, seq))
        xf = _final_call(xf, o_gdn, o_attn, npre, npost, w2, sc_conv_w, wb, wo, l, bsz,
                         min(FINAL_TILE, seq))
    return xf.reshape(bsz, seq, d)
```

```python
import functools
import math

import numpy as np
import jax
import jax.numpy as jnp
from jax import lax
from jax.experimental import pallas as pl
from jax.experimental.pallas import tpu as pltpu

F32 = jnp.float32
BF16 = jnp.bfloat16

LANES = 128
SUBLANES = 8
PACKED_SUBLANES = 16
VMEM_LIMIT = 56 * 1024 * 1024

BRANCH_WIDTH = 512
HEAD_DIM = 128
N_HEADS = BRANCH_WIDTH // HEAD_DIM
DIFF_QK_DIM = 64
GDN_CONV = 4
SC_CONV = 3
NORM_EPS = 1e-6
L2_EPS = 1e-6
NEG_BIG = -1e30
LOG2E = math.log2(math.e)
INV_BASE = 16
ONES_ROWS = PACKED_SUBLANES
FINAL_HALO = PACKED_SUBLANES

PROJ_TILE = 512
GDN_CHUNK = 128
ATTN_TILE = 1024
ATTN_COL_BLOCK = 256
FINAL_TILE = 512


def _dot(a, b):
    return jnp.dot(a.astype(BF16), b.astype(BF16), preferred_element_type=F32)


def _dot_nt(a, b):
    return lax.dot_general(a.astype(BF16), b.astype(BF16), (((1,), (1,)), ((), ())),
                           preferred_element_type=F32)


def _rms(x, w):
    return x * lax.rsqrt(jnp.mean(x * x, axis=-1, keepdims=True) + NORM_EPS) * w


def _sigmoid(x):
    return 0.5 * jnp.tanh(0.5 * x) + 0.5


def _silu(x):
    return x * _sigmoid(x)


def _softplus(x):
    return jnp.maximum(x, 0.0) + jnp.log(1.0 + jnp.exp(-jnp.abs(x)))


def _layer_spec(arr, layer, **kwargs):
    zeros = (0,) * (arr.ndim - 1)
    return pl.BlockSpec((None,) + arr.shape[1:], lambda *_: (layer,) + zeros, **kwargs)


def _proj_kernel(x_ref, nw_ref, w_ref, wba_ref, p_ref, ba_ref, wlead_scr, wrem_scr, *, col_chunk):
    @pl.when(pl.program_id(0) == 0)
    def _():
        wba = wba_ref[...]
        w_lead = wba.astype(BF16)
        wlead_scr[...] = w_lead
        wrem_scr[...] = (wba - w_lead.astype(F32)).astype(BF16)

    h = _rms(x_ref[...], nw_ref[...])
    hb = h.astype(BF16)
    for c in range(w_ref.shape[1] // col_chunk):
        cols = slice(c * col_chunk, (c + 1) * col_chunk)
        p_ref[:, cols] = jnp.dot(hb, w_ref[:, cols], preferred_element_type=F32).astype(BF16)
    h_rem = (h - hb.astype(F32)).astype(BF16)
    ba_ref[...] = (jnp.dot(hb, wlead_scr[...], preferred_element_type=F32)
                   + jnp.dot(hb, wrem_scr[...], preferred_element_type=F32)
                   + jnp.dot(h_rem, wlead_scr[...], preferred_element_type=F32))


def _proj_call(x, norm_w, w_all, w_in, layer, tile):
    n_tok, d = x.shape
    n_out = MIXER_COLS
    ba_block = BA_COL // LANES
    return pl.pallas_call(
        functools.partial(_proj_kernel, col_chunk=512),
        grid=(n_tok // tile,),
        in_specs=[pl.BlockSpec((tile, d), lambda i: (i, 0)),
                  _layer_spec(norm_w, layer),
                  pl.BlockSpec((None, d, n_out), lambda i: (layer, 0, 0)),
                  pl.BlockSpec((None, d, LANES), lambda i: (layer, 0, ba_block))],
        out_specs=[pl.BlockSpec((tile, n_out), lambda i: (i, 0)),
                   pl.BlockSpec((tile, LANES), lambda i: (i, 0))],
        out_shape=[jax.ShapeDtypeStruct((n_tok, n_out), BF16),
                   jax.ShapeDtypeStruct((n_tok, LANES), F32)],
        scratch_shapes=[pltpu.VMEM((d, LANES), BF16), pltpu.VMEM((d, LANES), BF16)],
        compiler_params=pltpu.CompilerParams(dimension_semantics=("arbitrary",),
                                             vmem_limit_bytes=VMEM_LIMIT),
        name="proj",
    )(x, norm_w, w_all, w_in)


BA_COL = 4 * BRANCH_WIDTH
BA_WIDTH = 2 * N_HEADS
REGROUP_CHUNK = BRANCH_WIDTH
MIXER_COLS = 6 * BRANCH_WIDTH
GATE_COLS = 6 * BRANCH_WIDTH


def _regroup_source(c):
    return jnp.where(c < 9, c + c // 3, jnp.where(c < 12, 3 + 4 * (c - 9), c))


def _regroup_kernel(a_ref, b_ref, o_ref):
    src = _regroup_source(pl.program_id(1))

    @pl.when(src < BA_COL // REGROUP_CHUNK)
    def _():
        o_ref[...] = a_ref[...].astype(BF16)

    @pl.when(src >= BA_COL // REGROUP_CHUNK)
    def _():
        o_ref[...] = jnp.concatenate([a_ref[:, BA_WIDTH:], b_ref[:, :BA_WIDTH]], axis=1).astype(BF16)


def _regroup_call(w_in):
    depth, d, n_in = w_in.shape
    n_chunks = (n_in - BA_WIDTH) // REGROUP_CHUNK
    per = REGROUP_CHUNK // LANES
    return pl.pallas_call(
        _regroup_kernel,
        grid=(depth, n_chunks),
        in_specs=[pl.BlockSpec((None, d, REGROUP_CHUNK), lambda l, c: (l, 0, _regroup_source(c))),
                  pl.BlockSpec((None, d, LANES), lambda l, c: (l, 0, per * (_regroup_source(c) + 1)))],
        out_specs=pl.BlockSpec((None, d, REGROUP_CHUNK), lambda l, c: (l, 0, c)),
        out_shape=jax.ShapeDtypeStruct((depth, d, n_in - BA_WIDTH), BF16),
        compiler_params=pltpu.CompilerParams(dimension_semantics=("arbitrary", "arbitrary"),
                                             vmem_limit_bytes=VMEM_LIMIT),
        name="regroup_w_in",
    )(w_in, w_in)


def _unit_lower_inverses(lms, xr):
    n = lms[0].shape[0]
    eye = jnp.where(xr == 0, 1.0, 0.0).astype(F32)
    base = xr < INV_BASE
    powers = [jnp.where(base, -lm, 0.0) for lm in lms]
    ts = [eye + p for p in powers]
    step = 2
    while step < INV_BASE:
        powers = [_dot(p, p) for p in powers]
        ts = [t + _dot(t, p) for t, p in zip(ts, powers)]
        step *= 2
    size = INV_BASE
    while size < n:
        band = (xr >= size) & (xr < 2 * size)
        xs = [_dot(jnp.where(band, lm, 0.0), t) for lm, t in zip(lms, ts)]
        ts = [t - _dot(t, x) for t, x in zip(ts, xs)]
        size *= 2
    return ts


def _gdn_kernel(p_ref, ba_ref, cw_ref, alog_ref, dtb_ref, nw_ref, o_ref, xbuf, state, *, chunk):
    t_idx = pl.program_id(0)
    halo = SUBLANES
    n_batch = p_ref.shape[0]
    chains = [(b, h) for b in range(n_batch) for h in range(N_HEADS)]

    @pl.when(t_idx == 0)
    def _():
        xbuf[:, 0:halo, :] = jnp.zeros((n_batch, halo, xbuf.shape[2]), F32)
        state[...] = jnp.zeros(state.shape, F32)

    xbuf[:, halo:, :] = p_ref[...].astype(F32)

    ri = lax.broadcasted_iota(jnp.int32, (chunk, chunk), 0)
    ci = lax.broadcasted_iota(jnp.int32, (chunk, chunk), 1)
    tril = ri >= ci
    strict = ri > ci
    xr = ri ^ ci
    tril_ones = jnp.where(tril, 1.0, 0.0).astype(F32)

    gbs, gcs, gc_ts = [], [], []
    for b in range(n_batch):
        pre = ba_ref[b]
        lane = lax.broadcasted_iota(jnp.int32, pre.shape, 1)
        g_all = -jnp.exp(alog_ref[...]) * _softplus(pre + dtb_ref[...])
        gb = jnp.where(lane < N_HEADS, _sigmoid(pre), jnp.where(lane < 2 * N_HEADS, g_all, 0.0))
        gc = jnp.dot(tril_ones, gb, preferred_element_type=F32, precision=lax.Precision.HIGHEST)
        gbs.append(gb)
        gcs.append(gc)
        gc_ts.append(gc.T)

    def conv_silu(b, col):
        acc = (cw_ref[GDN_CONV - 1:GDN_CONV, col:col + HEAD_DIM]
               * xbuf[b, halo:halo + chunk, col:col + HEAD_DIM])
        for tap in range(GDN_CONV - 1):
            shift = GDN_CONV - 1 - tap
            acc = acc + (cw_ref[tap:tap + 1, col:col + HEAD_DIM]
                         * xbuf[b, halo - shift:halo - shift + chunk, col:col + HEAD_DIM])
        return _silu(acc)

    def l2n(v):
        return v * lax.rsqrt(jnp.sum(v * v, axis=-1, keepdims=True) + L2_EPS)

    qs = [l2n(conv_silu(b, h * HEAD_DIM)) * (HEAD_DIM ** -0.5) for b, h in chains]
    ks = [l2n(conv_silu(b, BRANCH_WIDTH + h * HEAD_DIM)) for b, h in chains]
    vs = [conv_silu(b, 2 * BRANCH_WIDTH + h * HEAD_DIM) for b, h in chains]
    betas = [gbs[b][:, h:h + 1] for b, h in chains]
    gcols = [gcs[b][:, N_HEADS + h:N_HEADS + h + 1] for b, h in chains]
    grows = [gc_ts[b][N_HEADS + h:N_HEADS + h + 1, :] for b, h in chains]
    glasts = [gcs[b][chunk - 1:chunk, N_HEADS + h:N_HEADS + h + 1] for b, h in chains]
    decays = [jnp.exp(jnp.where(tril, gc - gr, NEG_BIG)) for gc, gr in zip(gcols, grows)]
    kbs = [k * beta for k, beta in zip(ks, betas)]
    lms = [jnp.where(strict, _dot_nt(kb, k) * d, 0.0) for kb, k, d in zip(kbs, ks, decays)]
    t_invs = _unit_lower_inverses(lms, xr)
    egs = [jnp.exp(gc) for gc in gcols]
    sols = [_dot(t, jnp.concatenate([v * beta, kb * eg], axis=1))
            for t, v, beta, kb, eg in zip(t_invs, vs, betas, kbs, egs)]
    qks = [_dot_nt(q, k) * d for q, k, d in zip(qs, ks, decays)]
    s_olds = [state[c] for c in range(len(chains))]
    wss = [_dot(jnp.concatenate([sol[:, HEAD_DIM:], q * eg], axis=0), s_old)
           for sol, q, eg, s_old in zip(sols, qs, egs, s_olds)]
    v_news = [sol[:, :HEAD_DIM] - ws[:chunk] for sol, ws in zip(sols, wss)]
    outs = [ws[chunk:] + _dot(qk, v_new) for ws, qk, v_new in zip(wss, qks, v_news)]
    for c, (b, h) in enumerate(chains):
        k_dec = ks[c] * jnp.exp(glasts[c] - gcols[c])
        state[c] = s_olds[c] * jnp.exp(glasts[c]) + _dot(k_dec.T, v_news[c])
        o_ref[b, :, h * HEAD_DIM:(h + 1) * HEAD_DIM] = _rms(outs[c], nw_ref[...]).astype(BF16)

    xbuf[:, 0:halo, :] = xbuf[:, chunk:chunk + halo, :]


def _gdn_call(p1, ba, conv_w, alog_rows, dtb_rows, norm_w, layer, batch, chunk):
    n_tok = p1.shape[0]
    seq = n_tok // batch
    width = 3 * BRANCH_WIDTH
    out = pl.pallas_call(
        functools.partial(_gdn_kernel, chunk=chunk),
        grid=(seq // chunk,),
        in_specs=[pl.BlockSpec((batch, chunk, width), lambda t: (0, t, 0)),
                  pl.BlockSpec((batch, chunk, LANES), lambda t: (0, t, 0)),
                  _layer_spec(conv_w, layer), _layer_spec(alog_rows, layer),
                  _layer_spec(dtb_rows, layer), _layer_spec(norm_w, layer)],
        out_specs=pl.BlockSpec((batch, chunk, BRANCH_WIDTH), lambda t: (0, t, 0)),
        out_shape=jax.ShapeDtypeStruct((batch, seq, BRANCH_WIDTH), BF16),
        scratch_shapes=[pltpu.VMEM((batch, chunk + SUBLANES, width), F32),
                        pltpu.VMEM((batch * N_HEADS, HEAD_DIM, HEAD_DIM), F32)],
        compiler_params=pltpu.CompilerParams(dimension_semantics=("arbitrary",),
                                             vmem_limit_bytes=VMEM_LIMIT),
        name="gdn",
    )(p1.reshape(batch, seq, p1.shape[1]), ba.reshape(batch, seq, LANES), conv_w, alog_rows,
      dtb_rows, norm_w)
    return out.reshape(n_tok, BRANCH_WIDTH)


def _attn_kernel(q_ref, qn_ref, k_ref, v_ref, lam_ref, li_ref, nw_ref, o_ref,
                 vt_scr, kpos_scr, qq_scr, qqn_scr, s_scr, smax_scr, m_scr, acc_scr, *, tile):
    head = pl.program_id(1)
    qi = pl.program_id(2)
    seq = k_ref.shape[0]
    r = 2 * tile
    slope = jnp.float32(LOG2E * 2.0 ** (-8.0 / N_HEADS))
    for hh in range(1, N_HEADS):
        slope = jnp.where(head == hh, jnp.float32(LOG2E * 2.0 ** (-8.0 * (hh + 1) / N_HEADS)), slope)

    @pl.when(qi == 0)
    def _():
        eye = jnp.where(lax.broadcasted_iota(jnp.int32, (HEAD_DIM, HEAD_DIM), 0)
                        == lax.broadcasted_iota(jnp.int32, (HEAD_DIM, HEAD_DIM), 1), 1.0, 0.0).astype(BF16)

        def transpose_block(blk, carry):
            st = pl.multiple_of(blk * tile, tile)
            vt_scr[0:HEAD_DIM, pl.ds(st, tile)] = lax.dot_general(
                eye, v_ref[pl.ds(st, tile), :], (((1,), (1,)), ((), ())),
                preferred_element_type=F32).astype(BF16)
            return carry
        lax.fori_loop(0, seq // tile, transpose_block, 0)
        vt_scr[HEAD_DIM:, :] = jnp.ones((vt_scr.shape[0] - HEAD_DIM, seq), BF16)
        key = lax.broadcasted_iota(jnp.int32, (tile, LANES), 0)
        ln = lax.broadcasted_iota(jnp.int32, (tile, LANES), 1)
        kpos = jnp.where(ln < 3, key // 16, jnp.where(ln < 6, key % 16, 0))
        kpos_scr[...] = kpos.astype(F32).astype(BF16)

    sv = jnp.full((1, LANES), slope, F32)
    s_hi = sv.astype(BF16).astype(F32)
    s_mid = (sv - s_hi).astype(BF16).astype(F32)
    s_lo = (sv - s_hi - s_mid).astype(BF16).astype(F32)
    ln1 = lax.broadcasted_iota(jnp.int32, (SUBLANES, LANES), 1)
    pieces = (s_hi * 16.0, s_mid * 16.0, s_lo * 16.0, s_hi, s_mid, s_lo)
    srow = jnp.zeros((SUBLANES, LANES), F32)
    for idx, piece in enumerate(pieces):
        srow = jnp.where(ln1 == idx, piece, srow)
    srow = jnp.broadcast_to(srow[0:1], (r, LANES)).astype(BF16)

    def stacked_queries(ref):
        q = ref[...].astype(F32) * (LOG2E * DIFF_QK_DIM ** -0.5)
        lane = lax.broadcasted_iota(jnp.int32, q.shape, 1)
        qq = jnp.concatenate([jnp.where(lane < DIFF_QK_DIM, q, 0.0),
                              jnp.where(lane >= DIFF_QK_DIM, q, 0.0)], axis=0).astype(BF16)
        return jnp.concatenate([qq, srow], axis=1)

    col_blocks = [slice(n * ATTN_COL_BLOCK, (n + 1) * ATTN_COL_BLOCK)
                  for n in range(r // ATTN_COL_BLOCK)]

    def score_block(qq_ref, key_tile, slot, cols, diagonal):
        start = pl.multiple_of(key_tile * tile, tile)
        n_keys = cols.start % tile + ATTN_COL_BLOCK if diagonal else tile
        kt = jnp.concatenate([k_ref[pl.ds(start, n_keys), :], kpos_scr[0:n_keys, :]], axis=1)
        s = lax.dot_general(kt, qq_ref[cols, :], (((1,), (1,)), ((), ())),
                            preferred_element_type=F32)
        if diagonal:
            key = lax.broadcasted_iota(jnp.int32, s.shape, 0)
            qry = lax.broadcasted_iota(jnp.int32, s.shape, 1) + cols.start % tile
            s = jnp.where(key <= qry, s, NEG_BIG)
            if n_keys < tile:
                s_scr[slot, n_keys:, cols] = jnp.full((tile - n_keys, ATTN_COL_BLOCK), NEG_BIG, F32)
        s_scr[slot, 0:n_keys, cols] = s
        smax_scr[slot, :, cols] = jnp.max(s, axis=0, keepdims=True)

    def accumulate_block(key_tile, slot, cols):
        start = pl.multiple_of(key_tile * tile, tile)
        cj = slope * ((key_tile - qi) * tile).astype(F32)
        m_prev = m_scr[:, cols]
        m_new = jnp.maximum(m_prev, smax_scr[slot, :, cols] + cj)
        alpha = jnp.exp2(m_prev - m_new)
        p = jnp.exp2((s_scr[slot, :, cols] - (m_new - cj)).astype(BF16))
        acc_scr[:, cols] = alpha * acc_scr[:, cols] + jnp.dot(
            vt_scr[:, pl.ds(start, tile)], p, preferred_element_type=F32)
        m_scr[:, cols] = m_new

    qq_scr[...] = stacked_queries(q_ref)

    @pl.when(qi == 0)
    def _():
        for cols in col_blocks:
            score_block(qq_scr, 0, 0, cols, True)

    m_scr[...] = jnp.full(m_scr.shape, NEG_BIG, F32)
    acc_scr[...] = jnp.zeros(acc_scr.shape, F32)

    first_slot = ((qi * (qi + 1)) // 2) % 2

    def body(t, carry):
        key_tile = jnp.where(t == 0, qi, t - 1)
        for slot in range(2):
            @pl.when((first_slot + t) % 2 == slot)
            def _():
                for cols in col_blocks:
                    score_block(qq_scr, t, 1 - slot, cols, False)
                    accumulate_block(key_tile, slot, cols)
        return carry

    lax.fori_loop(0, qi, body, 0)

    last_tile = jnp.where(qi == 0, 0, qi - 1)
    next_q = jnp.minimum(qi + 1, seq // tile - 1)
    for slot in range(2):
        @pl.when((first_slot + qi) % 2 == slot)
        def _():
            qqn_scr[...] = stacked_queries(qn_ref)
            for cols in col_blocks:
                score_block(qqn_scr, next_q, 1 - slot, cols, True)
                accumulate_block(last_tile, slot, cols)

    lv = lam_ref[...]
    lam_init = li_ref[:, 0:1]
    lam = (jnp.exp(jnp.sum(lv[0:1] * lv[1:2], axis=-1, keepdims=True))
           - jnp.exp(jnp.sum(lv[2:3] * lv[3:4], axis=-1, keepdims=True)) + lam_init)
    acc = acc_scr[...]
    o_all = acc[:HEAD_DIM] / acc[HEAD_DIM:HEAD_DIM + 1]
    o = (o_all[:, :tile] - lam * o_all[:, tile:]).T
    o_ref[...] = (_rms(o, nw_ref[...]) * (1.0 - lam_init)).astype(BF16)


def _attn_call(p1, lam_vecs, lam_init_rows, norm_w, layer, batch, tile):
    n_tok = p1.shape[0]
    seq = n_tok // batch
    n_q = seq // tile
    qcol = 3 * BRANCH_WIDTH // HEAD_DIM
    kcol = qcol + N_HEADS
    vcol = kcol + N_HEADS
    return pl.pallas_call(
        functools.partial(_attn_kernel, tile=tile),
        grid=(batch, N_HEADS, n_q),
        in_specs=[pl.BlockSpec((tile, HEAD_DIM), lambda b, h, i: (b * n_q + i, qcol + h)),
                  pl.BlockSpec((tile, HEAD_DIM),
                               lambda b, h, i: (b * n_q + jnp.minimum(i + 1, n_q - 1), qcol + h)),
                  pl.BlockSpec((seq, HEAD_DIM), lambda b, h, i: (b, kcol + h)),
                  pl.BlockSpec((seq, HEAD_DIM), lambda b, h, i: (b, vcol + h)),
                  _layer_spec(lam_vecs, layer), _layer_spec(lam_init_rows, layer),
                  _layer_spec(norm_w, layer)],
        out_specs=pl.BlockSpec((tile, HEAD_DIM), lambda b, h, i: (b * n_q + i, h)),
        out_shape=jax.ShapeDtypeStruct((n_tok, BRANCH_WIDTH), BF16),
        scratch_shapes=[pltpu.VMEM((HEAD_DIM + ONES_ROWS, seq), BF16),
                        pltpu.VMEM((tile, LANES), BF16),
                        pltpu.VMEM((2 * tile, 2 * LANES), BF16),
                        pltpu.VMEM((2 * tile, 2 * LANES), BF16),
                        pltpu.VMEM((2, tile, 2 * tile), F32),
                        pltpu.VMEM((2, 1, 2 * tile), F32),
                        pltpu.VMEM((1, 2 * tile), F32),
                        pltpu.VMEM((HEAD_DIM + ONES_ROWS, 2 * tile), F32)],
        compiler_params=pltpu.CompilerParams(
            dimension_semantics=("arbitrary", "arbitrary", "arbitrary"),
            vmem_limit_bytes=VMEM_LIMIT),
        name="diff_attn",
    )(p1, p1, p1, p1, lam_vecs, lam_init_rows, norm_w)


def _final_kernel(x_ref, xh_ref, og_ref, oa_ref, npre_ref, npost_ref, wg_ref, wm_ref, scw_ref,
                  wb_ref, wo_ref, out_ref, *, tiles_per_seq):
    i = pl.program_id(0)
    w = BRANCH_WIDTH
    x = x_ref[...]
    tile, d = x.shape
    hb = _rms(x, npre_ref[...]).astype(BF16)
    hh = _rms(xh_ref[...], npre_ref[...]).astype(BF16)

    def proj(cols):
        return jnp.dot(hb, wg_ref[:, cols], preferred_element_type=F32)

    cxe = jnp.dot(jnp.concatenate([hb, hh], axis=0), wg_ref[:, w:3 * w], preferred_element_type=F32)
    cxe = cxe[:, :w] * cxe[:, w:]
    cx = cxe[:tile]
    cxh = jnp.where(i % tiles_per_seq == 0, 0.0, cxe[tile:])
    rows = lax.broadcasted_iota(jnp.int32, (tile, w), 0)
    conv = scw_ref[SC_CONV - 1:SC_CONV, :] * cx
    for tap in range(SC_CONV - 1):
        shift = SC_CONV - 1 - tap
        shifted = pltpu.roll(cx, shift, axis=0)
        for r in range(shift):
            src = FINAL_HALO - shift + r
            shifted = jnp.where(rows == r, cxh[src:src + 1, :], shifted)
        conv = conv + scw_ref[tap:tap + 1, :] * shifted
    y_c = proj(slice(0, w)) * conv * _silu(proj(slice(5 * w, 6 * w)))
    y_a = og_ref[...].astype(F32) * _silu(proj(slice(3 * w, 4 * w)))
    y_b = oa_ref[...].astype(F32) * _silu(proj(slice(4 * w, 5 * w)))

    merged = jnp.zeros((tile, d), F32)
    for k, y in enumerate((y_a, y_b, y_c)):
        gate = _sigmoid(jnp.dot(hb, wm_ref[:, k * d:(k + 1) * d], preferred_element_type=F32))
        merged = merged + gate * jnp.dot(y.astype(BF16), wb_ref[k], preferred_element_type=F32)
    out = jnp.dot(merged.astype(BF16), wo_ref[...], preferred_element_type=F32)
    out_ref[...] = x + _rms(out, npost_ref[...])


def _final_call(x, o_gdn, o_attn, norm_pre, norm_post, w_all, sc_w, w_branch, w_out, layer, batch,
                tile):
    n_tok, d = x.shape
    tiles_per_seq = n_tok // batch // tile
    halo_blocks = tile // FINAL_HALO
    const = dict(pipeline_mode=pl.Buffered(1))
    assert MIXER_COLS == GATE_COLS
    return pl.pallas_call(
        functools.partial(_final_kernel, tiles_per_seq=tiles_per_seq),
        grid=(n_tok // tile,),
        in_specs=[pl.BlockSpec((tile, d), lambda i: (i, 0)),
                  pl.BlockSpec((FINAL_HALO, d), lambda i: (jnp.maximum(i * halo_blocks - 1, 0), 0)),
                  pl.BlockSpec((tile, BRANCH_WIDTH), lambda i: (i, 0)),
                  pl.BlockSpec((tile, BRANCH_WIDTH), lambda i: (i, 0)),
                  _layer_spec(norm_pre, layer), _layer_spec(norm_post, layer),
                  pl.BlockSpec((None, d, GATE_COLS), lambda i: (layer, 0, 1), **const),
                  pl.BlockSpec((None, d, GATE_COLS), lambda i: (layer, 0, 2), **const),
                  _layer_spec(sc_w, layer),
                  _layer_spec(w_branch, layer, **const), _layer_spec(w_out, layer, **const)],
        out_specs=pl.BlockSpec((tile, d), lambda i: (i, 0)),
        out_shape=jax.ShapeDtypeStruct((n_tok, d), F32),
        compiler_params=pltpu.CompilerParams(dimension_semantics=("arbitrary",),
                                             vmem_limit_bytes=VMEM_LIMIT),
        name="merge_out",
    )(x, x, o_gdn, o_attn, norm_pre, norm_post, w_all, w_all, sc_w, w_branch, w_out)


def _head_rows(vals):
    return jnp.pad(vals.astype(F32), ((0, 0), (N_HEADS, LANES - 2 * N_HEADS)))[:, None, :]


def kernel(x, norm_pre, norm_post, w_in, gdn_conv_w, gdn_a_log, gdn_dt_bias, gdn_norm_w,
           diff_lambda, diff_norm_w, sc_conv_w, w_branch, w_out):
    bsz, seq, d = x.shape
    depth = w_in.shape[0]
    w_all = _regroup_call(w_in)
    wb = w_branch.astype(BF16)
    wo = w_out.astype(BF16)
    npre = norm_pre[:, None, :]
    npost = norm_post[:, None, :]
    alog_rows = _head_rows(gdn_a_log)
    dtb_rows = _head_rows(gdn_dt_bias)
    gdn_nw = gdn_norm_w[:, None, :]
    diff_nw = diff_norm_w[:, None, :]
    lam_init_rows = jnp.asarray(np.broadcast_to(
        np.array([0.8 - 0.6 * math.exp(-0.3 * l) for l in range(depth)], np.float32)[:, None, None],
        (depth, 1, LANES)))
    xf = x.reshape(bsz * seq, d)
    for l in range(depth):
        p1, ba = _proj_call(xf, npre, w_all, w_in, l, min(PROJ_TILE, seq))
        o_gdn = _gdn_call(p1, ba, gdn_conv_w, alog_rows, dtb_rows, gdn_nw, l, bsz,
                          min(GDN_CHUNK, seq))
        o_attn = _attn_call(p1, diff_lambda, lam_init_rows, diff_nw, l, bsz,
                            min(ATTN_TILE, seq))
        xf = _final_call(xf, o_gdn, o_attn, npre, npost, w_all, sc_conv_w, wb, wo, l, bsz,
                         min(FINAL_TILE, seq))
    return xf.reshape(bsz, seq, d)
```

```python
import functools
import math

import numpy as np
import jax
import jax.numpy as jnp
from jax import lax
from jax.experimental import pallas as pl
from jax.experimental.pallas import tpu as pltpu

F32 = jnp.float32
BF16 = jnp.bfloat16

LANES = 128
SUBLANES = 8
PACKED_SUBLANES = 16
VMEM_LIMIT = 56 * 1024 * 1024

BRANCH_WIDTH = 512
HEAD_DIM = 128
N_HEADS = BRANCH_WIDTH // HEAD_DIM
DIFF_QK_DIM = 64
GDN_CONV = 4
SC_CONV = 3
NORM_EPS = 1e-6
L2_EPS = 1e-6
NEG_BIG = -1e30
LOG2E = math.log2(math.e)
INV_BASE = 16
ONES_ROWS = PACKED_SUBLANES
FINAL_HALO = PACKED_SUBLANES

PROJ_TILE = 512
GDN_CHUNK = 128
ATTN_TILE = 1024
ATTN_COL_BLOCK = 256
FINAL_TILE = 512


def _dot(a, b):
    return jnp.dot(a.astype(BF16), b.astype(BF16), preferred_element_type=F32)


def _dot_nt(a, b):
    return lax.dot_general(a.astype(BF16), b.astype(BF16), (((1,), (1,)), ((), ())),
                           preferred_element_type=F32)


def _rms(x, w):
    return x * lax.rsqrt(jnp.mean(x * x, axis=-1, keepdims=True) + NORM_EPS) * w


def _sigmoid(x):
    return 0.5 * jnp.tanh(0.5 * x) + 0.5


def _silu(x):
    return x * _sigmoid(x)


def _softplus(x):
    return jnp.maximum(x, 0.0) + jnp.log(1.0 + jnp.exp(-jnp.abs(x)))


def _layer_spec(arr, layer, **kwargs):
    zeros = (0,) * (arr.ndim - 1)
    return pl.BlockSpec((None,) + arr.shape[1:], lambda *_: (layer,) + zeros, **kwargs)


def _proj_kernel(x_ref, nw_ref, w_ref, wba_ref, p_ref, ba_ref, wpieces_scr, *, col_chunk):
    @pl.when(pl.program_id(0) == 0)
    def _():
        wba = wba_ref[...]
        w_lead = wba.astype(BF16).astype(F32)
        pieces = jnp.concatenate(
            [w_lead, wba - w_lead, jnp.zeros((LANES - 2 * BA_WIDTH, wba.shape[1]), F32)], axis=0)
        wpieces_scr[...] = pieces.astype(BF16)

    h = _rms(x_ref[...], nw_ref[...])
    hb = h.astype(BF16)
    for c in range(w_ref.shape[1] // col_chunk):
        cols = slice(c * col_chunk, (c + 1) * col_chunk)
        p_ref[:, cols] = jnp.dot(hb, w_ref[:, cols], preferred_element_type=F32).astype(BF16)
    h_rem = (h - hb.astype(F32)).astype(BF16)
    nt = (((1,), (1,)), ((), ()))
    pieces = wpieces_scr[...]
    s = (lax.dot_general(hb, pieces, nt, preferred_element_type=F32)
         + lax.dot_general(h_rem, pieces, nt, preferred_element_type=F32))
    ba_ref[...] = s + pltpu.roll(s, LANES - BA_WIDTH, axis=1)


def _proj_call(x, norm_w, w_all, w_in_t, layer, tile):
    n_tok, d = x.shape
    n_out = MIXER_COLS
    ba_block = BA_COL // BA_WIDTH
    return pl.pallas_call(
        functools.partial(_proj_kernel, col_chunk=512),
        grid=(n_tok // tile,),
        in_specs=[pl.BlockSpec((tile, d), lambda i: (i, 0)),
                  _layer_spec(norm_w, layer),
                  pl.BlockSpec((None, d, n_out), lambda i: (layer, 0, 0)),
                  pl.BlockSpec((None, BA_WIDTH, d), lambda i: (layer, ba_block, 0))],
        out_specs=[pl.BlockSpec((tile, n_out), lambda i: (i, 0)),
                   pl.BlockSpec((tile, LANES), lambda i: (i, 0))],
        out_shape=[jax.ShapeDtypeStruct((n_tok, n_out), BF16),
                   jax.ShapeDtypeStruct((n_tok, LANES), F32)],
        scratch_shapes=[pltpu.VMEM((LANES, d), BF16)],
        compiler_params=pltpu.CompilerParams(dimension_semantics=("arbitrary",),
                                             vmem_limit_bytes=VMEM_LIMIT),
        name="proj",
    )(x, norm_w, w_all, w_in_t)


BA_COL = 4 * BRANCH_WIDTH
BA_WIDTH = 2 * N_HEADS
REGROUP_CHUNK = BRANCH_WIDTH
MIXER_COLS = 6 * BRANCH_WIDTH
GATE_COLS = 6 * BRANCH_WIDTH


def _regroup_source(c):
    return jnp.where(c < 9, c + c // 3, jnp.where(c < 12, 3 + 4 * (c - 9), c))


def _regroup_kernel(a_ref, o_ref):
    o_ref[...] = a_ref[0].T.astype(BF16)


def _regroup_call(w_in_t):
    depth, n_in, d = w_in_t.shape
    n_chunks = (n_in - BA_WIDTH) // REGROUP_CHUNK

    def first_row(c):
        src = _regroup_source(c)
        tiles = src * (REGROUP_CHUNK // SUBLANES) + jnp.where(
            src >= BA_COL // REGROUP_CHUNK, BA_WIDTH // SUBLANES, 0)
        return tiles * SUBLANES

    return pl.pallas_call(
        _regroup_kernel,
        grid=(depth, n_chunks),
        in_specs=[pl.BlockSpec((pl.Element(1), pl.Element(REGROUP_CHUNK), pl.Element(d)),
                               lambda l, c: (l, first_row(c), 0))],
        out_specs=pl.BlockSpec((None, d, REGROUP_CHUNK), lambda l, c: (l, 0, c)),
        out_shape=jax.ShapeDtypeStruct((depth, d, n_in - BA_WIDTH), BF16),
        compiler_params=pltpu.CompilerParams(dimension_semantics=("arbitrary", "arbitrary"),
                                             vmem_limit_bytes=VMEM_LIMIT),
        name="regroup_w_in",
    )(w_in_t)


def _unit_lower_inverses(lms, xr):
    n = lms[0].shape[0]
    eye = jnp.where(xr == 0, 1.0, 0.0).astype(F32)
    base = xr < INV_BASE
    powers = [jnp.where(base, -lm, 0.0) for lm in lms]
    ts = [eye + p for p in powers]
    step = 2
    while step < INV_BASE:
        powers = [_dot(p, p) for p in powers]
        ts = [t + _dot(t, p) for t, p in zip(ts, powers)]
        step *= 2
    size = INV_BASE
    while size < n:
        band = (xr >= size) & (xr < 2 * size)
        xs = [_dot(jnp.where(band, lm, 0.0), t) for lm, t in zip(lms, ts)]
        ts = [t - _dot(t, x) for t, x in zip(ts, xs)]
        size *= 2
    return ts


def _gdn_kernel(p_ref, ba_ref, cw_ref, alog_ref, dtb_ref, nw_ref, o_ref, xbuf, state, *, chunk):
    t_idx = pl.program_id(0)
    halo = SUBLANES
    n_batch = p_ref.shape[0]
    chains = [(b, h) for b in range(n_batch) for h in range(N_HEADS)]

    @pl.when(t_idx == 0)
    def _():
        xbuf[:, 0:halo, :] = jnp.zeros((n_batch, halo, xbuf.shape[2]), F32)
        state[...] = jnp.zeros(state.shape, F32)

    xbuf[:, halo:, :] = p_ref[...].astype(F32)

    ri = lax.broadcasted_iota(jnp.int32, (chunk, chunk), 0)
    ci = lax.broadcasted_iota(jnp.int32, (chunk, chunk), 1)
    tril = ri >= ci
    strict = ri > ci
    xr = ri ^ ci
    tril_ones = jnp.where(tril, 1.0, 0.0).astype(F32)

    gbs, gcs, gc_ts = [], [], []
    for b in range(n_batch):
        pre = ba_ref[b]
        lane = lax.broadcasted_iota(jnp.int32, pre.shape, 1)
        g_all = -jnp.exp(alog_ref[...]) * _softplus(pre + dtb_ref[...])
        gb = jnp.where(lane < N_HEADS, _sigmoid(pre), jnp.where(lane < 2 * N_HEADS, g_all, 0.0))
        gc = jnp.dot(tril_ones, gb, preferred_element_type=F32, precision=lax.Precision.HIGHEST)
        gbs.append(gb)
        gcs.append(gc)
        gc_ts.append(gc.T)

    def conv_silu(b, col):
        acc = (cw_ref[GDN_CONV - 1:GDN_CONV, col:col + HEAD_DIM]
               * xbuf[b, halo:halo + chunk, col:col + HEAD_DIM])
        for tap in range(GDN_CONV - 1):
            shift = GDN_CONV - 1 - tap
            acc = acc + (cw_ref[tap:tap + 1, col:col + HEAD_DIM]
                         * xbuf[b, halo - shift:halo - shift + chunk, col:col + HEAD_DIM])
        return _silu(acc)

    def l2n(v):
        return v * lax.rsqrt(jnp.sum(v * v, axis=-1, keepdims=True) + L2_EPS)

    qs = [l2n(conv_silu(b, h * HEAD_DIM)) * (HEAD_DIM ** -0.5) for b, h in chains]
    ks = [l2n(conv_silu(b, BRANCH_WIDTH + h * HEAD_DIM)) for b, h in chains]
    vs = [conv_silu(b, 2 * BRANCH_WIDTH + h * HEAD_DIM) for b, h in chains]
    betas = [gbs[b][:, h:h + 1] for b, h in chains]
    gcols = [gcs[b][:, N_HEADS + h:N_HEADS + h + 1] for b, h in chains]
    grows = [gc_ts[b][N_HEADS + h:N_HEADS + h + 1, :] for b, h in chains]
    glasts = [gcs[b][chunk - 1:chunk, N_HEADS + h:N_HEADS + h + 1] for b, h in chains]
    decays = [jnp.exp(jnp.where(tril, gc - gr, NEG_BIG)) for gc, gr in zip(gcols, grows)]
    kbs = [k * beta for k, beta in zip(ks, betas)]
    lms = [jnp.where(strict, _dot_nt(kb, k) * d, 0.0) for kb, k, d in zip(kbs, ks, decays)]
    t_invs = _unit_lower_inverses(lms, xr)
    egs = [jnp.exp(gc) for gc in gcols]
    sols = [_dot(t, jnp.concatenate([v * beta, kb * eg], axis=1))
            for t, v, beta, kb, eg in zip(t_invs, vs, betas, kbs, egs)]
    qks = [_dot_nt(q, k) * d for q, k, d in zip(qs, ks, decays)]
    s_olds = [state[c] for c in range(len(chains))]
    wss = [_dot(jnp.concatenate([sol[:, HEAD_DIM:], q * eg], axis=0), s_old)
           for sol, q, eg, s_old in zip(sols, qs, egs, s_olds)]
    v_news = [sol[:, :HEAD_DIM] - ws[:chunk] for sol, ws in zip(sols, wss)]
    outs = [ws[chunk:] + _dot(qk, v_new) for ws, qk, v_new in zip(wss, qks, v_news)]
    for c, (b, h) in enumerate(chains):
        k_dec = ks[c] * jnp.exp(glasts[c] - gcols[c])
        state[c] = s_olds[c] * jnp.exp(glasts[c]) + _dot(k_dec.T, v_news[c])
        o_ref[b, :, h * HEAD_DIM:(h + 1) * HEAD_DIM] = _rms(outs[c], nw_ref[...]).astype(BF16)

    xbuf[:, 0:halo, :] = xbuf[:, chunk:chunk + halo, :]


def _gdn_call(p1, ba, conv_w, alog_rows, dtb_rows, norm_w, layer, batch, chunk):
    n_tok = p1.shape[0]
    seq = n_tok // batch
    width = 3 * BRANCH_WIDTH
    out = pl.pallas_call(
        functools.partial(_gdn_kernel, chunk=chunk),
        grid=(seq // chunk,),
        in_specs=[pl.BlockSpec((batch, chunk, width), lambda t: (0, t, 0)),
                  pl.BlockSpec((batch, chunk, LANES), lambda t: (0, t, 0)),
                  _layer_spec(conv_w, layer), _layer_spec(alog_rows, layer),
                  _layer_spec(dtb_rows, layer), _layer_spec(norm_w, layer)],
        out_specs=pl.BlockSpec((batch, chunk, BRANCH_WIDTH), lambda t: (0, t, 0)),
        out_shape=jax.ShapeDtypeStruct((batch, seq, BRANCH_WIDTH), BF16),
        scratch_shapes=[pltpu.VMEM((batch, chunk + SUBLANES, width), F32),
                        pltpu.VMEM((batch * N_HEADS, HEAD_DIM, HEAD_DIM), F32)],
        compiler_params=pltpu.CompilerParams(dimension_semantics=("arbitrary",),
                                             vmem_limit_bytes=VMEM_LIMIT),
        name="gdn",
    )(p1.reshape(batch, seq, p1.shape[1]), ba.reshape(batch, seq, LANES), conv_w, alog_rows,
      dtb_rows, norm_w)
    return out.reshape(n_tok, BRANCH_WIDTH)


def _attn_kernel(q_ref, qn_ref, k_ref, v_ref, lam_ref, li_ref, nw_ref, o_ref,
                 vt_scr, kpos_scr, qq_scr, qqn_scr, s_scr, smax_scr, m_scr, acc_scr, *, tile):
    head = pl.program_id(1)
    qi = pl.program_id(2)
    seq = k_ref.shape[0]
    r = 2 * tile
    slope = jnp.float32(LOG2E * 2.0 ** (-8.0 / N_HEADS))
    for hh in range(1, N_HEADS):
        slope = jnp.where(head == hh, jnp.float32(LOG2E * 2.0 ** (-8.0 * (hh + 1) / N_HEADS)), slope)

    @pl.when(qi == 0)
    def _():
        eye = jnp.where(lax.broadcasted_iota(jnp.int32, (HEAD_DIM, HEAD_DIM), 0)
                        == lax.broadcasted_iota(jnp.int32, (HEAD_DIM, HEAD_DIM), 1), 1.0, 0.0).astype(BF16)

        def transpose_block(blk, carry):
            st = pl.multiple_of(blk * tile, tile)
            vt_scr[0:HEAD_DIM, pl.ds(st, tile)] = lax.dot_general(
                eye, v_ref[pl.ds(st, tile), :], (((1,), (1,)), ((), ())),
                preferred_element_type=F32).astype(BF16)
            return carry
        lax.fori_loop(0, seq // tile, transpose_block, 0)
        vt_scr[HEAD_DIM:, :] = jnp.ones((vt_scr.shape[0] - HEAD_DIM, seq), BF16)
        key = lax.broadcasted_iota(jnp.int32, (tile, LANES), 0)
        ln = lax.broadcasted_iota(jnp.int32, (tile, LANES), 1)
        kpos = jnp.where(ln < 3, key // 16, jnp.where(ln < 6, key % 16, 0))
        kpos_scr[...] = kpos.astype(F32).astype(BF16)

    sv = jnp.full((1, LANES), slope, F32)
    s_hi = sv.astype(BF16).astype(F32)
    s_mid = (sv - s_hi).astype(BF16).astype(F32)
    s_lo = (sv - s_hi - s_mid).astype(BF16).astype(F32)
    ln1 = lax.broadcasted_iota(jnp.int32, (SUBLANES, LANES), 1)
    pieces = (s_hi * 16.0, s_mid * 16.0, s_lo * 16.0, s_hi, s_mid, s_lo)
    srow = jnp.zeros((SUBLANES, LANES), F32)
    for idx, piece in enumerate(pieces):
        srow = jnp.where(ln1 == idx, piece, srow)
    srow = jnp.broadcast_to(srow[0:1], (r, LANES)).astype(BF16)

    def stacked_queries(ref):
        q = ref[...].astype(F32) * (LOG2E * DIFF_QK_DIM ** -0.5)
        lane = lax.broadcasted_iota(jnp.int32, q.shape, 1)
        qq = jnp.concatenate([jnp.where(lane < DIFF_QK_DIM, q, 0.0),
                              jnp.where(lane >= DIFF_QK_DIM, q, 0.0)], axis=0).astype(BF16)
        return jnp.concatenate([qq, srow], axis=1)

    col_blocks = [slice(n * ATTN_COL_BLOCK, (n + 1) * ATTN_COL_BLOCK)
                  for n in range(r // ATTN_COL_BLOCK)]

    def score_block(qq_ref, key_tile, slot, cols, diagonal):
        start = pl.multiple_of(key_tile * tile, tile)
        n_keys = cols.start % tile + ATTN_COL_BLOCK if diagonal else tile
        kt = jnp.concatenate([k_ref[pl.ds(start, n_keys), :], kpos_scr[0:n_keys, :]], axis=1)
        s = lax.dot_general(kt, qq_ref[cols, :], (((1,), (1,)), ((), ())),
                            preferred_element_type=F32)
        if diagonal:
            key = lax.broadcasted_iota(jnp.int32, s.shape, 0)
            qry = lax.broadcasted_iota(jnp.int32, s.shape, 1) + cols.start % tile
            s = jnp.where(key <= qry, s, NEG_BIG)
            if n_keys < tile:
                s_scr[slot, n_keys:, cols] = jnp.full((tile - n_keys, ATTN_COL_BLOCK), NEG_BIG, F32)
        s_scr[slot, 0:n_keys, cols] = s
        smax_scr[slot, :, cols] = jnp.max(s, axis=0, keepdims=True)

    def accumulate_block(key_tile, slot, cols):
        start = pl.multiple_of(key_tile * tile, tile)
        cj = slope * ((key_tile - qi) * tile).astype(F32)
        m_prev = m_scr[:, cols]
        m_new = jnp.maximum(m_prev, smax_scr[slot, :, cols] + cj)
        alpha = jnp.exp2(m_prev - m_new)
        p = jnp.exp2((s_scr[slot, :, cols] - (m_new - cj)).astype(BF16))
        acc_scr[:, cols] = alpha * acc_scr[:, cols] + jnp.dot(
            vt_scr[:, pl.ds(start, tile)], p, preferred_element_type=F32)
        m_scr[:, cols] = m_new

    qq_scr[...] = stacked_queries(q_ref)

    @pl.when(qi == 0)
    def _():
        for cols in col_blocks:
            score_block(qq_scr, 0, 0, cols, True)

    m_scr[...] = jnp.full(m_scr.shape, NEG_BIG, F32)
    acc_scr[...] = jnp.zeros(acc_scr.shape, F32)

    first_slot = ((qi * (qi + 1)) // 2) % 2

    def body(t, carry):
        key_tile = jnp.where(t == 0, qi, t - 1)
        for slot in range(2):
            @pl.when((first_slot + t) % 2 == slot)
            def _():
                for cols in col_blocks:
                    score_block(qq_scr, t, 1 - slot, cols, False)
                    accumulate_block(key_tile, slot, cols)
        return carry

    lax.fori_loop(0, qi, body, 0)

    last_tile = jnp.where(qi == 0, 0, qi - 1)
    next_q = jnp.minimum(qi + 1, seq // tile - 1)
    for slot in range(2):
        @pl.when((first_slot + qi) % 2 == slot)
        def _():
            qqn_scr[...] = stacked_queries(qn_ref)
            for cols in col_blocks:
                score_block(qqn_scr, next_q, 1 - slot, cols, True)
                accumulate_block(last_tile, slot, cols)

    lv = lam_ref[...]
    lam_init = li_ref[:, 0:1]
    lam = (jnp.exp(jnp.sum(lv[0:1] * lv[1:2], axis=-1, keepdims=True))
           - jnp.exp(jnp.sum(lv[2:3] * lv[3:4], axis=-1, keepdims=True)) + lam_init)
    acc = acc_scr[...]
    o_all = acc[:HEAD_DIM] * (1.0 / acc[HEAD_DIM:HEAD_DIM + 1])
    o = (o_all[:, :tile] - lam * o_all[:, tile:]).T
    o_ref[...] = (_rms(o, nw_ref[...]) * (1.0 - lam_init)).astype(BF16)


def _attn_call(p1, lam_vecs, lam_init_rows, norm_w, layer, batch, tile):
    n_tok = p1.shape[0]
    seq = n_tok // batch
    n_q = seq // tile
    qcol = 3 * BRANCH_WIDTH // HEAD_DIM
    kcol = qcol + N_HEADS
    vcol = kcol + N_HEADS
    return pl.pallas_call(
        functools.partial(_attn_kernel, tile=tile),
        grid=(batch, N_HEADS, n_q),
        in_specs=[pl.BlockSpec((tile, HEAD_DIM), lambda b, h, i: (b * n_q + i, qcol + h)),
                  pl.BlockSpec((tile, HEAD_DIM),
                               lambda b, h, i: (b * n_q + jnp.minimum(i + 1, n_q - 1), qcol + h)),
                  pl.BlockSpec((seq, HEAD_DIM), lambda b, h, i: (b, kcol + h)),
                  pl.BlockSpec((seq, HEAD_DIM), lambda b, h, i: (b, vcol + h)),
                  _layer_spec(lam_vecs, layer), _layer_spec(lam_init_rows, layer),
                  _layer_spec(norm_w, layer)],
        out_specs=pl.BlockSpec((tile, HEAD_DIM), lambda b, h, i: (b * n_q + i, h)),
        out_shape=jax.ShapeDtypeStruct((n_tok, BRANCH_WIDTH), BF16),
        scratch_shapes=[pltpu.VMEM((HEAD_DIM + ONES_ROWS, seq), BF16),
                        pltpu.VMEM((tile, LANES), BF16),
                        pltpu.VMEM((2 * tile, 2 * LANES), BF16),
                        pltpu.VMEM((2 * tile, 2 * LANES), BF16),
                        pltpu.VMEM((2, tile, 2 * tile), F32),
                        pltpu.VMEM((2, 1, 2 * tile), F32),
                        pltpu.VMEM((1, 2 * tile), F32),
                        pltpu.VMEM((HEAD_DIM + ONES_ROWS, 2 * tile), F32)],
        compiler_params=pltpu.CompilerParams(
            dimension_semantics=("arbitrary", "arbitrary", "arbitrary"),
            vmem_limit_bytes=VMEM_LIMIT),
        name="diff_attn",
    )(p1, p1, p1, p1, lam_vecs, lam_init_rows, norm_w)


def _final_kernel(x_ref, xh_ref, og_ref, oa_ref, npre_ref, npost_ref, wg_ref, wm_ref, scw_ref,
                  wb_ref, wo_ref, out_ref, *, tiles_per_seq):
    i = pl.program_id(0)
    w = BRANCH_WIDTH
    x = x_ref[...]
    tile, d = x.shape
    hb = _rms(x, npre_ref[...]).astype(BF16)
    hh = _rms(xh_ref[...], npre_ref[...]).astype(BF16)

    def proj(cols):
        return jnp.dot(hb, wg_ref[:, cols], preferred_element_type=F32)

    cxe = jnp.dot(jnp.concatenate([hb, hh], axis=0), wg_ref[:, w:3 * w], preferred_element_type=F32)
    cxe = cxe[:, :w] * cxe[:, w:]
    cx = cxe[:tile]
    cxh = jnp.where(i % tiles_per_seq == 0, 0.0, cxe[tile:])
    rows = lax.broadcasted_iota(jnp.int32, (tile, w), 0)
    conv = scw_ref[SC_CONV - 1:SC_CONV, :] * cx
    for tap in range(SC_CONV - 1):
        shift = SC_CONV - 1 - tap
        shifted = pltpu.roll(cx, shift, axis=0)
        for r in range(shift):
            src = FINAL_HALO - shift + r
            shifted = jnp.where(rows == r, cxh[src:src + 1, :], shifted)
        conv = conv + scw_ref[tap:tap + 1, :] * shifted
    y_c = proj(slice(0, w)) * conv * _silu(proj(slice(5 * w, 6 * w)))
    y_a = og_ref[...].astype(F32) * _silu(proj(slice(3 * w, 4 * w)))
    y_b = oa_ref[...].astype(F32) * _silu(proj(slice(4 * w, 5 * w)))

    merged = jnp.zeros((tile, d), F32)
    for k, y in enumerate((y_a, y_b, y_c)):
        gate = _sigmoid(jnp.dot(hb, wm_ref[:, k * d:(k + 1) * d], preferred_element_type=F32))
        merged = merged + gate * jnp.dot(y.astype(BF16), wb_ref[k], preferred_element_type=F32)
    out = jnp.dot(merged.astype(BF16), wo_ref[...], preferred_element_type=F32)
    out_ref[...] = x + _rms(out, npost_ref[...])


def _final_call(x, o_gdn, o_attn, norm_pre, norm_post, w_all, sc_w, w_branch, w_out, layer, batch,
                tile):
    n_tok, d = x.shape
    tiles_per_seq = n_tok // batch // tile
    halo_blocks = tile // FINAL_HALO
    const = dict(pipeline_mode=pl.Buffered(1))
    assert MIXER_COLS == GATE_COLS
    return pl.pallas_call(
        functools.partial(_final_kernel, tiles_per_seq=tiles_per_seq),
        grid=(n_tok // tile,),
        in_specs=[pl.BlockSpec((tile, d), lambda i: (i, 0)),
                  pl.BlockSpec((FINAL_HALO, d), lambda i: (jnp.maximum(i * halo_blocks - 1, 0), 0)),
                  pl.BlockSpec((tile, BRANCH_WIDTH), lambda i: (i, 0)),
                  pl.BlockSpec((tile, BRANCH_WIDTH), lambda i: (i, 0)),
                  _layer_spec(norm_pre, layer), _layer_spec(norm_post, layer),
                  pl.BlockSpec((None, d, GATE_COLS), lambda i: (layer, 0, 1), **const),
                  pl.BlockSpec((None, d, GATE_COLS), lambda i: (layer, 0, 2), **const),
                  _layer_spec(sc_w, layer),
                  _layer_spec(w_branch, layer, **const), _layer_spec(w_out, layer, **const)],
        out_specs=pl.BlockSpec((tile, d), lambda i: (i, 0)),
        out_shape=jax.ShapeDtypeStruct((n_tok, d), F32),
        compiler_params=pltpu.CompilerParams(dimension_semantics=("arbitrary",),
                                             vmem_limit_bytes=VMEM_LIMIT),
        name="merge_out",
    )(x, x, o_gdn, o_attn, norm_pre, norm_post, w_all, w_all, sc_w, w_branch, w_out)


def _head_rows(vals):
    return jnp.pad(vals.astype(F32), ((0, 0), (N_HEADS, LANES - 2 * N_HEADS)))[:, None, :]


def kernel(x, norm_pre, norm_post, w_in, gdn_conv_w, gdn_a_log, gdn_dt_bias, gdn_norm_w,
           diff_lambda, diff_norm_w, sc_conv_w, w_branch, w_out):
    bsz, seq, d = x.shape
    depth = w_in.shape[0]
    w_in_t = jnp.swapaxes(w_in, 1, 2)
    w_all = _regroup_call(w_in_t)
    wb = w_branch.astype(BF16)
    wo = w_out.astype(BF16)
    npre = norm_pre[:, None, :]
    npost = norm_post[:, None, :]
    alog_rows = _head_rows(gdn_a_log)
    dtb_rows = _head_rows(gdn_dt_bias)
    gdn_nw = gdn_norm_w[:, None, :]
    diff_nw = diff_norm_w[:, None, :]
    lam_init_rows = jnp.asarray(np.broadcast_to(
        np.array([0.8 - 0.6 * math.exp(-0.3 * l) for l in range(depth)], np.float32)[:, None, None],
        (depth, 1, LANES)))
    xf = x.reshape(bsz * seq, d)
    for l in range(depth):
        p1, ba = _proj_call(xf, npre, w_all, w_in_t, l, min(PROJ_TILE, seq))
        o_gdn = _gdn_call(p1, ba, gdn_conv_w, alog_rows, dtb_rows, gdn_nw, l, bsz,
                          min(GDN_CHUNK, seq))
        o_attn = _attn_call(p1, diff_lambda, lam_init_rows, diff_nw, l, bsz,
                            min(ATTN_TILE, seq))
        xf = _final_call(xf, o_gdn, o_attn, npre, npost, w_all, sc_conv_w, wb, wo, l, bsz,
                         min(FINAL_TILE, seq))
    return xf.reshape(bsz, seq, d)
```

```python
import functools
import math

import numpy as np
import jax
import jax.numpy as jnp
from jax import lax
from jax.experimental import pallas as pl
from jax.experimental.pallas import tpu as pltpu

F32 = jnp.float32
BF16 = jnp.bfloat16

LANES = 128
SUBLANES = 8
PACKED_SUBLANES = 16
VMEM_LIMIT = 56 * 1024 * 1024

BRANCH_WIDTH = 512
HEAD_DIM = 128
N_HEADS = BRANCH_WIDTH // HEAD_DIM
DIFF_QK_DIM = 64
GDN_CONV = 4
SC_CONV = 3
NORM_EPS = 1e-6
L2_EPS = 1e-6
NEG_BIG = -1e30
LOG2E = math.log2(math.e)
INV_BASE = 16
ONES_ROWS = PACKED_SUBLANES
FINAL_HALO = PACKED_SUBLANES

PROJ_TILE = 512
GDN_CHUNK = 128
ATTN_TILE = 1024
ATTN_COL_BLOCK = 256
FINAL_TILE = 512


def _dot(a, b):
    return jnp.dot(a.astype(BF16), b.astype(BF16), preferred_element_type=F32)


def _dot_nt(a, b):
    return lax.dot_general(a.astype(BF16), b.astype(BF16), (((1,), (1,)), ((), ())),
                           preferred_element_type=F32)


def _rms(x, w):
    return x * lax.rsqrt(jnp.mean(x * x, axis=-1, keepdims=True) + NORM_EPS) * w


def _sigmoid(x):
    return 0.5 * jnp.tanh(0.5 * x) + 0.5


def _silu(x):
    return x * _sigmoid(x)


def _softplus(x):
    return jnp.maximum(x, 0.0) + jnp.log(1.0 + jnp.exp(-jnp.abs(x)))


def _layer_spec(arr, layer, **kwargs):
    zeros = (0,) * (arr.ndim - 1)
    return pl.BlockSpec((None,) + arr.shape[1:], lambda *_: (layer,) + zeros, **kwargs)


def _proj_kernel(x_ref, nw_ref, w_ref, wba_ref, p_ref, ba_ref, wpieces_scr, *, col_chunk):
    @pl.when(pl.program_id(0) == 0)
    def _():
        wba = wba_ref[...]
        w_lead = wba.astype(BF16).astype(F32)
        pieces = jnp.concatenate(
            [w_lead, wba - w_lead, jnp.zeros((LANES - 2 * BA_WIDTH, wba.shape[1]), F32)], axis=0)
        wpieces_scr[...] = pieces.astype(BF16)

    h = _rms(x_ref[...], nw_ref[...])
    hb = h.astype(BF16)
    for c in range(w_ref.shape[1] // col_chunk):
        cols = slice(c * col_chunk, (c + 1) * col_chunk)
        p_ref[:, cols] = jnp.dot(hb, w_ref[:, cols], preferred_element_type=F32).astype(BF16)
    h_rem = (h - hb.astype(F32)).astype(BF16)
    nt = (((1,), (1,)), ((), ()))
    pieces = wpieces_scr[...]
    s = (lax.dot_general(hb, pieces, nt, preferred_element_type=F32)
         + lax.dot_general(h_rem, pieces, nt, preferred_element_type=F32))
    ba_ref[...] = s + pltpu.roll(s, LANES - BA_WIDTH, axis=1)


def _proj_call(x, norm_w, w_all, w_in_t, layer, tile):
    n_tok, d = x.shape
    n_out = MIXER_COLS
    ba_block = BA_COL // BA_WIDTH
    return pl.pallas_call(
        functools.partial(_proj_kernel, col_chunk=512),
        grid=(n_tok // tile,),
        in_specs=[pl.BlockSpec((tile, d), lambda i: (i, 0)),
                  _layer_spec(norm_w, layer),
                  pl.BlockSpec((None, d, n_out), lambda i: (layer, 0, 0)),
                  pl.BlockSpec((None, BA_WIDTH, d), lambda i: (layer, ba_block, 0))],
        out_specs=[pl.BlockSpec((tile, n_out), lambda i: (i, 0)),
                   pl.BlockSpec((tile, LANES), lambda i: (i, 0))],
        out_shape=[jax.ShapeDtypeStruct((n_tok, n_out), BF16),
                   jax.ShapeDtypeStruct((n_tok, LANES), F32)],
        scratch_shapes=[pltpu.VMEM((LANES, d), BF16)],
        compiler_params=pltpu.CompilerParams(dimension_semantics=("arbitrary",),
                                             vmem_limit_bytes=VMEM_LIMIT),
        name="proj",
    )(x, norm_w, w_all, w_in_t)


BA_COL = 4 * BRANCH_WIDTH
BA_WIDTH = 2 * N_HEADS
REGROUP_CHUNK = BRANCH_WIDTH
MIXER_COLS = 6 * BRANCH_WIDTH
GATE_COLS = 6 * BRANCH_WIDTH


def _regroup_source(c):
    return jnp.where(c < 9, c + c // 3, jnp.where(c < 12, 3 + 4 * (c - 9), c))


def _regroup_kernel(a_ref, o_ref):
    o_ref[...] = a_ref[0].T.astype(BF16)


def _regroup_call(w_in_t):
    depth, n_in, d = w_in_t.shape
    n_chunks = (n_in - BA_WIDTH) // REGROUP_CHUNK

    def first_row(c):
        src = _regroup_source(c)
        tiles = src * (REGROUP_CHUNK // SUBLANES) + jnp.where(
            src >= BA_COL // REGROUP_CHUNK, BA_WIDTH // SUBLANES, 0)
        return tiles * SUBLANES

    return pl.pallas_call(
        _regroup_kernel,
        grid=(depth, n_chunks),
        in_specs=[pl.BlockSpec((pl.Element(1), pl.Element(REGROUP_CHUNK), pl.Element(d)),
                               lambda l, c: (l, first_row(c), 0))],
        out_specs=pl.BlockSpec((None, d, REGROUP_CHUNK), lambda l, c: (l, 0, c)),
        out_shape=jax.ShapeDtypeStruct((depth, d, n_in - BA_WIDTH), BF16),
        compiler_params=pltpu.CompilerParams(dimension_semantics=("arbitrary", "arbitrary"),
                                             vmem_limit_bytes=VMEM_LIMIT),
        name="regroup_w_in",
    )(w_in_t)


def _unit_lower_inverses(lms, xr):
    n = lms[0].shape[0]
    eye = jnp.where(xr == 0, 1.0, 0.0).astype(F32)
    base = xr < INV_BASE
    powers = [jnp.where(base, -lm, 0.0) for lm in lms]
    ts = [eye + p for p in powers]
    step = 2
    while step < INV_BASE:
        powers = [_dot(p, p) for p in powers]
        ts = [t + _dot(t, p) for t, p in zip(ts, powers)]
        step *= 2
    size = INV_BASE
    while size < n:
        band = (xr >= size) & (xr < 2 * size)
        xs = [_dot(jnp.where(band, lm, 0.0), t) for lm, t in zip(lms, ts)]
        ts = [t - _dot(t, x) for t, x in zip(ts, xs)]
        size *= 2
    return ts


def _gdn_kernel(p_ref, ba_ref, cw_ref, alog_ref, dtb_ref, nw_ref, o_ref, xbuf, state, *, chunk):
    t_idx = pl.program_id(0)
    halo = SUBLANES
    n_batch = p_ref.shape[0]
    chains = [(b, h) for b in range(n_batch) for h in range(N_HEADS)]

    @pl.when(t_idx == 0)
    def _():
        xbuf[:, 0:halo, :] = jnp.zeros((n_batch, halo, xbuf.shape[2]), F32)
        state[...] = jnp.zeros(state.shape, F32)

    xbuf[:, halo:, :] = p_ref[...].astype(F32)

    ri = lax.broadcasted_iota(jnp.int32, (chunk, chunk), 0)
    ci = lax.broadcasted_iota(jnp.int32, (chunk, chunk), 1)
    tril = ri >= ci
    strict = ri > ci
    xr = ri ^ ci
    tril_ones = jnp.where(tril, 1.0, 0.0).astype(F32)

    gbs, gcs, gc_ts = [], [], []
    for b in range(n_batch):
        pre = ba_ref[b]
        lane = lax.broadcasted_iota(jnp.int32, pre.shape, 1)
        g_all = -jnp.exp(alog_ref[...]) * _softplus(pre + dtb_ref[...])
        gb = jnp.where(lane < N_HEADS, _sigmoid(pre), jnp.where(lane < 2 * N_HEADS, g_all, 0.0))
        gc = jnp.dot(tril_ones, gb, preferred_element_type=F32, precision=lax.Precision.HIGHEST)
        gbs.append(gb)
        gcs.append(gc)
        gc_ts.append(gc.T)

    def conv_silu(b, col):
        acc = (cw_ref[GDN_CONV - 1:GDN_CONV, col:col + HEAD_DIM]
               * xbuf[b, halo:halo + chunk, col:col + HEAD_DIM])
        for tap in range(GDN_CONV - 1):
            shift = GDN_CONV - 1 - tap
            acc = acc + (cw_ref[tap:tap + 1, col:col + HEAD_DIM]
                         * xbuf[b, halo - shift:halo - shift + chunk, col:col + HEAD_DIM])
        return _silu(acc)

    def l2n(v):
        return v * lax.rsqrt(jnp.sum(v * v, axis=-1, keepdims=True) + L2_EPS)

    qs = [l2n(conv_silu(b, h * HEAD_DIM)) * (HEAD_DIM ** -0.5) for b, h in chains]
    ks = [l2n(conv_silu(b, BRANCH_WIDTH + h * HEAD_DIM)) for b, h in chains]
    vs = [conv_silu(b, 2 * BRANCH_WIDTH + h * HEAD_DIM) for b, h in chains]
    betas = [gbs[b][:, h:h + 1] for b, h in chains]
    gcols = [gcs[b][:, N_HEADS + h:N_HEADS + h + 1] for b, h in chains]
    grows = [gc_ts[b][N_HEADS + h:N_HEADS + h + 1, :] for b, h in chains]
    glasts = [gcs[b][chunk - 1:chunk, N_HEADS + h:N_HEADS + h + 1] for b, h in chains]
    decays = [jnp.exp(jnp.where(tril, gc - gr, NEG_BIG)) for gc, gr in zip(gcols, grows)]
    kbs = [k * beta for k, beta in zip(ks, betas)]
    lms = [jnp.where(strict, _dot_nt(kb, k) * d, 0.0) for kb, k, d in zip(kbs, ks, decays)]
    t_invs = _unit_lower_inverses(lms, xr)
    egs = [jnp.exp(gc) for gc in gcols]
    sols = [_dot(t, jnp.concatenate([v * beta, kb * eg], axis=1))
            for t, v, beta, kb, eg in zip(t_invs, vs, betas, kbs, egs)]
    qks = [_dot_nt(q, k) * d for q, k, d in zip(qs, ks, decays)]
    s_olds = [state[c] for c in range(len(chains))]
    wss = [_dot(jnp.concatenate([sol[:, HEAD_DIM:], q * eg], axis=0), s_old)
           for sol, q, eg, s_old in zip(sols, qs, egs, s_olds)]
    v_news = [sol[:, :HEAD_DIM] - ws[:chunk] for sol, ws in zip(sols, wss)]
    outs = [ws[chunk:] + _dot(qk, v_new) for ws, qk, v_new in zip(wss, qks, v_news)]
    for c, (b, h) in enumerate(chains):
        k_dec = ks[c] * jnp.exp(glasts[c] - gcols[c])
        state[c] = s_olds[c] * jnp.exp(glasts[c]) + _dot(k_dec.T, v_news[c])
        o_ref[b, :, h * HEAD_DIM:(h + 1) * HEAD_DIM] = _rms(outs[c], nw_ref[...]).astype(BF16)

    xbuf[:, 0:halo, :] = xbuf[:, chunk:chunk + halo, :]


def _gdn_call(p1, ba, conv_w, alog_rows, dtb_rows, norm_w, layer, batch, chunk):
    n_tok = p1.shape[0]
    seq = n_tok // batch
    width = 3 * BRANCH_WIDTH
    out = pl.pallas_call(
        functools.partial(_gdn_kernel, chunk=chunk),
        grid=(seq // chunk,),
        in_specs=[pl.BlockSpec((batch, chunk, width), lambda t: (0, t, 0)),
                  pl.BlockSpec((batch, chunk, LANES), lambda t: (0, t, 0)),
                  _layer_spec(conv_w, layer), _layer_spec(alog_rows, layer),
                  _layer_spec(dtb_rows, layer), _layer_spec(norm_w, layer)],
        out_specs=pl.BlockSpec((batch, chunk, BRANCH_WIDTH), lambda t: (0, t, 0)),
        out_shape=jax.ShapeDtypeStruct((batch, seq, BRANCH_WIDTH), BF16),
        scratch_shapes=[pltpu.VMEM((batch, chunk + SUBLANES, width), F32),
                        pltpu.VMEM((batch * N_HEADS, HEAD_DIM, HEAD_DIM), F32)],
        compiler_params=pltpu.CompilerParams(dimension_semantics=("arbitrary",),
                                             vmem_limit_bytes=VMEM_LIMIT),
        name="gdn",
    )(p1.reshape(batch, seq, p1.shape[1]), ba.reshape(batch, seq, LANES), conv_w, alog_rows,
      dtb_rows, norm_w)
    return out.reshape(n_tok, BRANCH_WIDTH)


def _attn_kernel(q_ref, qn_ref, k_ref, v_ref, lam_ref, li_ref, nw_ref, o_ref,
                 vt_scr, kpos_scr, qq_scr, qqn_scr, s_scr, smax_scr, m_scr, acc_scr, *, tile):
    head = pl.program_id(1)
    qi = pl.program_id(2)
    seq = k_ref.shape[0]
    r = 2 * tile
    slope = jnp.float32(LOG2E * 2.0 ** (-8.0 / N_HEADS))
    for hh in range(1, N_HEADS):
        slope = jnp.where(head == hh, jnp.float32(LOG2E * 2.0 ** (-8.0 * (hh + 1) / N_HEADS)), slope)

    @pl.when(qi == 0)
    def _():
        eye = jnp.where(lax.broadcasted_iota(jnp.int32, (HEAD_DIM, HEAD_DIM), 0)
                        == lax.broadcasted_iota(jnp.int32, (HEAD_DIM, HEAD_DIM), 1), 1.0, 0.0).astype(BF16)

        def transpose_block(blk, carry):
            st = pl.multiple_of(blk * tile, tile)
            vt_scr[0:HEAD_DIM, pl.ds(st, tile)] = lax.dot_general(
                eye, v_ref[pl.ds(st, tile), :], (((1,), (1,)), ((), ())),
                preferred_element_type=F32).astype(BF16)
            return carry
        lax.fori_loop(0, seq // tile, transpose_block, 0)
        vt_scr[HEAD_DIM:, :] = jnp.ones((vt_scr.shape[0] - HEAD_DIM, seq), BF16)
        key = lax.broadcasted_iota(jnp.int32, (tile, LANES), 0)
        ln = lax.broadcasted_iota(jnp.int32, (tile, LANES), 1)
        kpos = jnp.where(ln < 3, key // 16, jnp.where(ln < 6, key % 16, 0))
        kpos_scr[...] = kpos.astype(F32).astype(BF16)

    sv = jnp.full((1, LANES), slope, F32)
    s_hi = sv.astype(BF16).astype(F32)
    s_mid = (sv - s_hi).astype(BF16).astype(F32)
    s_lo = (sv - s_hi - s_mid).astype(BF16).astype(F32)
    ln1 = lax.broadcasted_iota(jnp.int32, (SUBLANES, LANES), 1)
    pieces = (s_hi * 16.0, s_mid * 16.0, s_lo * 16.0, s_hi, s_mid, s_lo)
    srow = jnp.zeros((SUBLANES, LANES), F32)
    for idx, piece in enumerate(pieces):
        srow = jnp.where(ln1 == idx, piece, srow)
    srow = jnp.broadcast_to(srow[0:1], (r, LANES)).astype(BF16)

    def stacked_queries(ref):
        q = ref[...].astype(F32) * (LOG2E * DIFF_QK_DIM ** -0.5)
        lane = lax.broadcasted_iota(jnp.int32, q.shape, 1)
        qq = jnp.concatenate([jnp.where(lane < DIFF_QK_DIM, q, 0.0),
                              jnp.where(lane >= DIFF_QK_DIM, q, 0.0)], axis=0).astype(BF16)
        return jnp.concatenate([qq, srow], axis=1)

    col_blocks = [slice(n * ATTN_COL_BLOCK, (n + 1) * ATTN_COL_BLOCK)
                  for n in range(r // ATTN_COL_BLOCK)]

    def score_block(qq_ref, key_tile, slot, cols, diagonal):
        start = pl.multiple_of(key_tile * tile, tile)
        n_keys = cols.start % tile + ATTN_COL_BLOCK if diagonal else tile
        kt = jnp.concatenate([k_ref[pl.ds(start, n_keys), :], kpos_scr[0:n_keys, :]], axis=1)
        s = lax.dot_general(kt, qq_ref[cols, :], (((1,), (1,)), ((), ())),
                            preferred_element_type=F32)
        if diagonal:
            key = lax.broadcasted_iota(jnp.int32, s.shape, 0)
            qry = lax.broadcasted_iota(jnp.int32, s.shape, 1) + cols.start % tile
            s = jnp.where(key <= qry, s, NEG_BIG)
            if n_keys < tile:
                s_scr[slot, n_keys:, cols] = jnp.full((tile - n_keys, ATTN_COL_BLOCK), NEG_BIG, F32)
        s_scr[slot, 0:n_keys, cols] = s
        smax_scr[slot, :, cols] = jnp.max(s, axis=0, keepdims=True)

    def accumulate_block(key_tile, slot, cols, diagonal=False):
        start = pl.multiple_of(key_tile * tile, tile)
        n_keys = cols.start % tile + ATTN_COL_BLOCK if diagonal else tile
        cj = slope * ((key_tile - qi) * tile).astype(F32)
        m_prev = m_scr[:, cols]
        m_new = jnp.maximum(m_prev, smax_scr[slot, :, cols] + cj)
        alpha = jnp.exp2(m_prev - m_new)
        p = jnp.exp2((s_scr[slot, 0:n_keys, cols] - (m_new - cj)).astype(BF16))
        acc_scr[:, cols] = alpha * acc_scr[:, cols] + jnp.dot(
            vt_scr[:, pl.ds(start, n_keys)], p, preferred_element_type=F32)
        m_scr[:, cols] = m_new

    qq_scr[...] = stacked_queries(q_ref)

    @pl.when(qi == 0)
    def _():
        for cols in col_blocks:
            score_block(qq_scr, 0, 0, cols, True)

    m_scr[...] = jnp.full(m_scr.shape, NEG_BIG, F32)
    acc_scr[...] = jnp.zeros(acc_scr.shape, F32)

    first_slot = ((qi * (qi + 1)) // 2) % 2

    for slot in range(2):
        @pl.when(jnp.logical_and(qi > 0, first_slot == slot))
        def _():
            for cols in col_blocks:
                score_block(qq_scr, 0, 1 - slot, cols, False)
                accumulate_block(qi, slot, cols, diagonal=True)

    def body(t, carry):
        for slot in range(2):
            @pl.when((first_slot + t) % 2 == slot)
            def _():
                for cols in col_blocks:
                    score_block(qq_scr, t, 1 - slot, cols, False)
                    accumulate_block(t - 1, slot, cols)
        return carry

    lax.fori_loop(1, qi, body, 0)

    last_tile = jnp.where(qi == 0, 0, qi - 1)
    next_q = jnp.minimum(qi + 1, seq // tile - 1)
    for slot in range(2):
        @pl.when((first_slot + qi) % 2 == slot)
        def _():
            qqn_scr[...] = stacked_queries(qn_ref)
            for cols in col_blocks:
                score_block(qqn_scr, next_q, 1 - slot, cols, True)
                accumulate_block(last_tile, slot, cols)

    lv = lam_ref[...]
    lam_init = li_ref[:, 0:1]
    lam = (jnp.exp(jnp.sum(lv[0:1] * lv[1:2], axis=-1, keepdims=True))
           - jnp.exp(jnp.sum(lv[2:3] * lv[3:4], axis=-1, keepdims=True)) + lam_init)
    acc = acc_scr[...]
    o_all = acc[:HEAD_DIM] * (1.0 / acc[HEAD_DIM:HEAD_DIM + 1])
    o = (o_all[:, :tile] - lam * o_all[:, tile:]).T
    o_ref[...] = (_rms(o, nw_ref[...]) * (1.0 - lam_init)).astype(BF16)


def _attn_call(p1, lam_vecs, lam_init_rows, norm_w, layer, batch, tile):
    n_tok = p1.shape[0]
    seq = n_tok // batch
    n_q = seq // tile
    qcol = 3 * BRANCH_WIDTH // HEAD_DIM
    kcol = qcol + N_HEADS
    vcol = kcol + N_HEADS
    return pl.pallas_call(
        functools.partial(_attn_kernel, tile=tile),
        grid=(batch, N_HEADS, n_q),
        in_specs=[pl.BlockSpec((tile, HEAD_DIM), lambda b, h, i: (b * n_q + i, qcol + h)),
                  pl.BlockSpec((tile, HEAD_DIM),
                               lambda b, h, i: (b * n_q + jnp.minimum(i + 1, n_q - 1), qcol + h)),
                  pl.BlockSpec((seq, HEAD_DIM), lambda b, h, i: (b, kcol + h)),
                  pl.BlockSpec((seq, HEAD_DIM), lambda b, h, i: (b, vcol + h)),
                  _layer_spec(lam_vecs, layer), _layer_spec(lam_init_rows, layer),
                  _layer_spec(norm_w, layer)],
        out_specs=pl.BlockSpec((tile, HEAD_DIM), lambda b, h, i: (b * n_q + i, h)),
        out_shape=jax.ShapeDtypeStruct((n_tok, BRANCH_WIDTH), BF16),
        scratch_shapes=[pltpu.VMEM((HEAD_DIM + ONES_ROWS, seq), BF16),
                        pltpu.VMEM((tile, LANES), BF16),
                        pltpu.VMEM((2 * tile, 2 * LANES), BF16),
                        pltpu.VMEM((2 * tile, 2 * LANES), BF16),
                        pltpu.VMEM((2, tile, 2 * tile), F32),
                        pltpu.VMEM((2, 1, 2 * tile), F32),
                        pltpu.VMEM((1, 2 * tile), F32),
                        pltpu.VMEM((HEAD_DIM + ONES_ROWS, 2 * tile), F32)],
        compiler_params=pltpu.CompilerParams(
            dimension_semantics=("arbitrary", "arbitrary", "arbitrary"),
            vmem_limit_bytes=VMEM_LIMIT),
        name="diff_attn",
    )(p1, p1, p1, p1, lam_vecs, lam_init_rows, norm_w)


def _final_kernel(x_ref, xh_ref, og_ref, oa_ref, npre_ref, npost_ref, wg_ref, wm_ref, scw_ref,
                  wb_ref, wo_ref, out_ref, *, tiles_per_seq):
    i = pl.program_id(0)
    w = BRANCH_WIDTH
    x = x_ref[...]
    tile, d = x.shape
    hb = _rms(x, npre_ref[...]).astype(BF16)
    hh = _rms(xh_ref[...], npre_ref[...]).astype(BF16)

    def proj(cols):
        return jnp.dot(hb, wg_ref[:, cols], preferred_element_type=F32)

    cxe = jnp.dot(jnp.concatenate([hb, hh], axis=0), wg_ref[:, w:3 * w], preferred_element_type=F32)
    cxe = cxe[:, :w] * cxe[:, w:]
    cx = cxe[:tile]
    cxh = jnp.where(i % tiles_per_seq == 0, 0.0, cxe[tile:])
    rows = lax.broadcasted_iota(jnp.int32, (tile, w), 0)
    conv = scw_ref[SC_CONV - 1:SC_CONV, :] * cx
    for tap in range(SC_CONV - 1):
        shift = SC_CONV - 1 - tap
        shifted = pltpu.roll(cx, shift, axis=0)
        for r in range(shift):
            src = FINAL_HALO - shift + r
            shifted = jnp.where(rows == r, cxh[src:src + 1, :], shifted)
        conv = conv + scw_ref[tap:tap + 1, :] * shifted
    y_c = proj(slice(0, w)) * conv * _silu(proj(slice(5 * w, 6 * w)))
    y_a = og_ref[...].astype(F32) * _silu(proj(slice(3 * w, 4 * w)))
    y_b = oa_ref[...].astype(F32) * _silu(proj(slice(4 * w, 5 * w)))

    merged = jnp.zeros((tile, d), F32)
    for k, y in enumerate((y_a, y_b, y_c)):
        gate = _sigmoid(jnp.dot(hb, wm_ref[:, k * d:(k + 1) * d], preferred_element_type=F32))
        merged = merged + gate * jnp.dot(y.astype(BF16), wb_ref[k], preferred_element_type=F32)
    out = jnp.dot(merged.astype(BF16), wo_ref[...], preferred_element_type=F32)
    out_ref[...] = x + _rms(out, npost_ref[...])


def _final_call(x, o_gdn, o_attn, norm_pre, norm_post, w_all, sc_w, w_branch, w_out, layer, batch,
                tile):
    n_tok, d = x.shape
    tiles_per_seq = n_tok // batch // tile
    halo_blocks = tile // FINAL_HALO
    const = dict(pipeline_mode=pl.Buffered(1))
    assert MIXER_COLS == GATE_COLS
    return pl.pallas_call(
        functools.partial(_final_kernel, tiles_per_seq=tiles_per_seq),
        grid=(n_tok // tile,),
        in_specs=[pl.BlockSpec((tile, d), lambda i: (i, 0)),
                  pl.BlockSpec((FINAL_HALO, d), lambda i: (jnp.maximum(i * halo_blocks - 1, 0), 0)),
                  pl.BlockSpec((tile, BRANCH_WIDTH), lambda i: (i, 0)),
                  pl.BlockSpec((tile, BRANCH_WIDTH), lambda i: (i, 0)),
                  _layer_spec(norm_pre, layer), _layer_spec(norm_post, layer),
                  pl.BlockSpec((None, d, GATE_COLS), lambda i: (layer, 0, 1), **const),
                  pl.BlockSpec((None, d, GATE_COLS), lambda i: (layer, 0, 2), **const),
                  _layer_spec(sc_w, layer),
                  _layer_spec(w_branch, layer, **const), _layer_spec(w_out, layer, **const)],
        out_specs=pl.BlockSpec((tile, d), lambda i: (i, 0)),
        out_shape=jax.ShapeDtypeStruct((n_tok, d), F32),
        compiler_params=pltpu.CompilerParams(dimension_semantics=("arbitrary",),
                                             vmem_limit_bytes=VMEM_LIMIT),
        name="merge_out",
    )(x, x, o_gdn, o_attn, norm_pre, norm_post, w_all, w_all, sc_w, w_branch, w_out)


def _head_rows(vals):
    return jnp.pad(vals.astype(F32), ((0, 0), (N_HEADS, LANES - 2 * N_HEADS)))[:, None, :]


def kernel(x, norm_pre, norm_post, w_in, gdn_conv_w, gdn_a_log, gdn_dt_bias, gdn_norm_w,
           diff_lambda, diff_norm_w, sc_conv_w, w_branch, w_out):
    bsz, seq, d = x.shape
    depth = w_in.shape[0]
    w_in_t = jnp.swapaxes(w_in, 1, 2)
    w_all = _regroup_call(w_in_t)
    wb = w_branch.astype(BF16)
    wo = w_out.astype(BF16)
    npre = norm_pre[:, None, :]
    npost = norm_post[:, None, :]
    alog_rows = _head_rows(gdn_a_log)
    dtb_rows = _head_rows(gdn_dt_bias)
    gdn_nw = gdn_norm_w[:, None, :]
    diff_nw = diff_norm_w[:, None, :]
    lam_init_rows = jnp.asarray(np.broadcast_to(
        np.array([0.8 - 0.6 * math.exp(-0.3 * l) for l in range(depth)], np.float32)[:, None, None],
        (depth, 1, LANES)))
    xf = x.reshape(bsz * seq, d)
    for l in range(depth):
        p1, ba = _proj_call(xf, npre, w_all, w_in_t, l, min(PROJ_TILE, seq))
        o_gdn = _gdn_call(p1, ba, gdn_conv_w, alog_rows, dtb_rows, gdn_nw, l, bsz,
                          min(GDN_CHUNK, seq))
        o_attn = _attn_call(p1, diff_lambda, lam_init_rows, diff_nw, l, bsz,
                            min(ATTN_TILE, seq))
        xf = _final_call(xf, o_gdn, o_attn, npre, npost, w_all, sc_conv_w, wb, wo, l, bsz,
                         min(FINAL_TILE, seq))
    return xf.reshape(bsz, seq, d)
```

```python
import functools
import math

import numpy as np
import jax
import jax.numpy as jnp
from jax import lax
from jax.experimental import pallas as pl
from jax.experimental.pallas import tpu as pltpu

F32 = jnp.float32
BF16 = jnp.bfloat16

LANES = 128
SUBLANES = 8
PACKED_SUBLANES = 16
VMEM_LIMIT = 56 * 1024 * 1024

BRANCH_WIDTH = 512
HEAD_DIM = 128
N_HEADS = BRANCH_WIDTH // HEAD_DIM
DIFF_QK_DIM = 64
GDN_CONV = 4
SC_CONV = 3
NORM_EPS = 1e-6
L2_EPS = 1e-6
NEG_BIG = -1e30
LOG2E = math.log2(math.e)
INV_BASE = 16
ONES_ROWS = PACKED_SUBLANES
FINAL_HALO = PACKED_SUBLANES

PROJ_TILE = 1024
GDN_CHUNK = 128
ATTN_TILE = 1024
ATTN_COL_BLOCK = 256
FINAL_TILE = 512
MERGE_COL_BLOCK = 512


def _dot(a, b):
    return jnp.dot(a.astype(BF16), b.astype(BF16), preferred_element_type=F32)


def _dot_nt(a, b):
    return lax.dot_general(a.astype(BF16), b.astype(BF16), (((1,), (1,)), ((), ())),
                           preferred_element_type=F32)


def _rms(x, w):
    return x * lax.rsqrt(jnp.mean(x * x, axis=-1, keepdims=True) + NORM_EPS) * w


def _sigmoid(x):
    return 0.5 * jnp.tanh(0.5 * x) + 0.5


def _silu(x):
    u = 0.5 * x
    return u * (jnp.tanh(u) + 1.0)


def _softplus(x):
    return jnp.maximum(x, 0.0) + jnp.log(1.0 + jnp.exp(-jnp.abs(x)))


def _layer_spec(arr, layer, **kwargs):
    zeros = (0,) * (arr.ndim - 1)
    return pl.BlockSpec((None,) + arr.shape[1:], lambda *_: (layer,) + zeros, **kwargs)


def _proj_kernel(x_ref, nw_ref, w_ref, wba_ref, p_ref, ba_ref, wpieces_scr, *, col_chunk):
    @pl.when(pl.program_id(0) == 0)
    def _():
        wba = wba_ref[...]
        w_lead = wba.astype(BF16).astype(F32)
        pieces = jnp.concatenate(
            [w_lead, wba - w_lead, jnp.zeros((LANES - 2 * BA_WIDTH, wba.shape[1]), F32)], axis=0)
        wpieces_scr[...] = pieces.astype(BF16)

    h = _rms(x_ref[...], nw_ref[...])
    hb = h.astype(BF16)
    for c in range(w_ref.shape[1] // col_chunk):
        cols = slice(c * col_chunk, (c + 1) * col_chunk)
        p_ref[:, cols] = jnp.dot(hb, w_ref[:, cols], preferred_element_type=F32).astype(BF16)
    h_rem = (h - hb.astype(F32)).astype(BF16)
    nt = (((1,), (1,)), ((), ()))
    pieces = wpieces_scr[...]
    s = (lax.dot_general(hb, pieces, nt, preferred_element_type=F32)
         + lax.dot_general(h_rem, pieces, nt, preferred_element_type=F32))
    ba_ref[...] = s + pltpu.roll(s, LANES - BA_WIDTH, axis=1)


def _proj_call(x, norm_w, w_all, w_in_t, layer, tile):
    n_tok, d = x.shape
    n_out = MIXER_COLS
    ba_block = BA_COL // BA_WIDTH
    return pl.pallas_call(
        functools.partial(_proj_kernel, col_chunk=512),
        grid=(n_tok // tile,),
        in_specs=[pl.BlockSpec((tile, d), lambda i: (i, 0)),
                  _layer_spec(norm_w, layer),
                  pl.BlockSpec((None, d, n_out), lambda i: (layer, 0, 0)),
                  pl.BlockSpec((None, BA_WIDTH, d), lambda i: (layer, ba_block, 0))],
        out_specs=[pl.BlockSpec((tile, n_out), lambda i: (i, 0)),
                   pl.BlockSpec((tile, LANES), lambda i: (i, 0))],
        out_shape=[jax.ShapeDtypeStruct((n_tok, n_out), BF16),
                   jax.ShapeDtypeStruct((n_tok, LANES), F32)],
        scratch_shapes=[pltpu.VMEM((LANES, d), BF16)],
        compiler_params=pltpu.CompilerParams(dimension_semantics=("arbitrary",),
                                             vmem_limit_bytes=VMEM_LIMIT),
        name="proj",
    )(x, norm_w, w_all, w_in_t)


BA_COL = 4 * BRANCH_WIDTH
BA_WIDTH = 2 * N_HEADS
REGROUP_CHUNK = BRANCH_WIDTH
MIXER_COLS = 6 * BRANCH_WIDTH
GATE_COLS = 6 * BRANCH_WIDTH


def _regroup_source(c):
    return jnp.where(c < 9, c + c // 3, jnp.where(c < 12, 3 + 4 * (c - 9), c))


def _regroup_kernel(a_ref, o_ref):
    o_ref[...] = a_ref[0].T.astype(BF16)


def _regroup_call(w_in_t):
    depth, n_in, d = w_in_t.shape
    n_chunks = (n_in - BA_WIDTH) // REGROUP_CHUNK

    def first_row(c):
        src = _regroup_source(c)
        tiles = src * (REGROUP_CHUNK // SUBLANES) + jnp.where(
            src >= BA_COL // REGROUP_CHUNK, BA_WIDTH // SUBLANES, 0)
        return tiles * SUBLANES

    return pl.pallas_call(
        _regroup_kernel,
        grid=(depth, n_chunks),
        in_specs=[pl.BlockSpec((pl.Element(1), pl.Element(REGROUP_CHUNK), pl.Element(d)),
                               lambda l, c: (l, first_row(c), 0))],
        out_specs=pl.BlockSpec((None, d, REGROUP_CHUNK), lambda l, c: (l, 0, c)),
        out_shape=jax.ShapeDtypeStruct((depth, d, n_in - BA_WIDTH), BF16),
        compiler_params=pltpu.CompilerParams(dimension_semantics=("arbitrary", "arbitrary"),
                                             vmem_limit_bytes=VMEM_LIMIT),
        name="regroup_w_in",
    )(w_in_t)


def _unit_lower_inverses(lms, xr):
    n = lms[0].shape[0]
    eye = jnp.where(xr == 0, 1.0, 0.0).astype(F32)
    base = xr < INV_BASE
    powers = [jnp.where(base, -lm, 0.0) for lm in lms]
    ts = [eye + p for p in powers]
    step = 2
    while step < INV_BASE:
        powers = [_dot(p, p) for p in powers]
        ts = [t + _dot(t, p) for t, p in zip(ts, powers)]
        step *= 2
    size = INV_BASE
    while size < n:
        band = (xr >= size) & (xr < 2 * size)
        xs = [_dot(jnp.where(band, lm, 0.0), t) for lm, t in zip(lms, ts)]
        ts = [t - _dot(t, x) for t, x in zip(ts, xs)]
        size *= 2
    return ts


def _gdn_kernel(p_ref, ba_ref, cw_ref, alog_ref, dtb_ref, nw_ref, o_ref, xbuf, state, *, chunk):
    t_idx = pl.program_id(0)
    halo = SUBLANES
    n_batch = p_ref.shape[0]
    chains = [(b, h) for b in range(n_batch) for h in range(N_HEADS)]

    @pl.when(t_idx == 0)
    def _():
        xbuf[:, 0:halo, :] = jnp.zeros((n_batch, halo, xbuf.shape[2]), F32)
        state[...] = jnp.zeros(state.shape, F32)

    xbuf[:, halo:, :] = p_ref[...].astype(F32)

    ri = lax.broadcasted_iota(jnp.int32, (chunk, chunk), 0)
    ci = lax.broadcasted_iota(jnp.int32, (chunk, chunk), 1)
    tril = ri >= ci
    strict = ri > ci
    xr = ri ^ ci
    tril_ones = jnp.where(tril, 1.0, 0.0).astype(F32)

    gbs, gcs, gc_ts = [], [], []
    for b in range(n_batch):
        pre = ba_ref[b]
        lane = lax.broadcasted_iota(jnp.int32, pre.shape, 1)
        g_all = -jnp.exp(alog_ref[...]) * _softplus(pre + dtb_ref[...])
        gb = jnp.where(lane < N_HEADS, _sigmoid(pre), jnp.where(lane < 2 * N_HEADS, g_all, 0.0))
        gc = jnp.dot(tril_ones, gb, preferred_element_type=F32, precision=lax.Precision.HIGHEST)
        gbs.append(gb)
        gcs.append(gc)
        gc_ts.append(gc.T)

    def conv_silu(b, col):
        acc = (cw_ref[GDN_CONV - 1:GDN_CONV, col:col + HEAD_DIM]
               * xbuf[b, halo:halo + chunk, col:col + HEAD_DIM])
        for tap in range(GDN_CONV - 1):
            shift = GDN_CONV - 1 - tap
            acc = acc + (cw_ref[tap:tap + 1, col:col + HEAD_DIM]
                         * xbuf[b, halo - shift:halo - shift + chunk, col:col + HEAD_DIM])
        return _silu(acc)

    def l2n(v):
        return v * lax.rsqrt(jnp.sum(v * v, axis=-1, keepdims=True) + L2_EPS)

    qs = [l2n(conv_silu(b, h * HEAD_DIM)) * (HEAD_DIM ** -0.5) for b, h in chains]
    ks = [l2n(conv_silu(b, BRANCH_WIDTH + h * HEAD_DIM)) for b, h in chains]
    vs = [conv_silu(b, 2 * BRANCH_WIDTH + h * HEAD_DIM) for b, h in chains]
    betas = [gbs[b][:, h:h + 1] for b, h in chains]
    gcols = [gcs[b][:, N_HEADS + h:N_HEADS + h + 1] for b, h in chains]
    grows = [gc_ts[b][N_HEADS + h:N_HEADS + h + 1, :] for b, h in chains]
    glasts = [gcs[b][chunk - 1:chunk, N_HEADS + h:N_HEADS + h + 1] for b, h in chains]
    decays = [jnp.exp(jnp.where(tril, gc - gr, NEG_BIG)) for gc, gr in zip(gcols, grows)]
    kbs = [k * beta for k, beta in zip(ks, betas)]
    lms = [jnp.where(strict, _dot_nt(kb, k) * d, 0.0) for kb, k, d in zip(kbs, ks, decays)]
    t_invs = _unit_lower_inverses(lms, xr)
    egs = [jnp.exp(gc) for gc in gcols]
    sols = [_dot(t, jnp.concatenate([v * beta, kb * eg], axis=1))
            for t, v, beta, kb, eg in zip(t_invs, vs, betas, kbs, egs)]
    qks = [_dot_nt(q, k) * d for q, k, d in zip(qs, ks, decays)]
    s_olds = [state[c] for c in range(len(chains))]
    wss = [_dot(jnp.concatenate([sol[:, HEAD_DIM:], q * eg], axis=0), s_old)
           for sol, q, eg, s_old in zip(sols, qs, egs, s_olds)]
    v_news = [sol[:, :HEAD_DIM] - ws[:chunk] for sol, ws in zip(sols, wss)]
    outs = [ws[chunk:] + _dot(qk, v_new) for ws, qk, v_new in zip(wss, qks, v_news)]
    for c, (b, h) in enumerate(chains):
        k_dec = ks[c] * jnp.exp(glasts[c] - gcols[c])
        state[c] = s_olds[c] * jnp.exp(glasts[c]) + _dot(k_dec.T, v_news[c])
        o_ref[b, :, h * HEAD_DIM:(h + 1) * HEAD_DIM] = _rms(outs[c], nw_ref[...]).astype(BF16)

    xbuf[:, 0:halo, :] = xbuf[:, chunk:chunk + halo, :]


def _gdn_call(p1, ba, conv_w, alog_rows, dtb_rows, norm_w, layer, batch, chunk):
    n_tok = p1.shape[0]
    seq = n_tok // batch
    width = 3 * BRANCH_WIDTH
    out = pl.pallas_call(
        functools.partial(_gdn_kernel, chunk=chunk),
        grid=(seq // chunk,),
        in_specs=[pl.BlockSpec((batch, chunk, width), lambda t: (0, t, 0)),
                  pl.BlockSpec((batch, chunk, LANES), lambda t: (0, t, 0)),
                  _layer_spec(conv_w, layer), _layer_spec(alog_rows, layer),
                  _layer_spec(dtb_rows, layer), _layer_spec(norm_w, layer)],
        out_specs=pl.BlockSpec((batch, chunk, BRANCH_WIDTH), lambda t: (0, t, 0)),
        out_shape=jax.ShapeDtypeStruct((batch, seq, BRANCH_WIDTH), BF16),
        scratch_shapes=[pltpu.VMEM((batch, chunk + SUBLANES, width), F32),
                        pltpu.VMEM((batch * N_HEADS, HEAD_DIM, HEAD_DIM), F32)],
        compiler_params=pltpu.CompilerParams(dimension_semantics=("arbitrary",),
                                             vmem_limit_bytes=VMEM_LIMIT),
        name="gdn",
    )(p1.reshape(batch, seq, p1.shape[1]), ba.reshape(batch, seq, LANES), conv_w, alog_rows,
      dtb_rows, norm_w)
    return out.reshape(n_tok, BRANCH_WIDTH)


def _attn_kernel(q_ref, qn_ref, k_ref, v_ref, lam_ref, li_ref, nw_ref, o_ref,
                 vt_scr, kpos_scr, qq_scr, qqn_scr, s_scr, smax_scr, m_scr, acc_scr, *, tile):
    head = pl.program_id(1)
    qi = pl.program_id(2)
    seq = k_ref.shape[0]
    r = 2 * tile
    slope = jnp.float32(LOG2E * 2.0 ** (-8.0 / N_HEADS))
    for hh in range(1, N_HEADS):
        slope = jnp.where(head == hh, jnp.float32(LOG2E * 2.0 ** (-8.0 * (hh + 1) / N_HEADS)), slope)

    @pl.when(qi == 0)
    def _():
        eye = jnp.where(lax.broadcasted_iota(jnp.int32, (HEAD_DIM, HEAD_DIM), 0)
                        == lax.broadcasted_iota(jnp.int32, (HEAD_DIM, HEAD_DIM), 1), 1.0, 0.0).astype(BF16)

        def transpose_block(blk, carry):
            st = pl.multiple_of(blk * tile, tile)
            vt_scr[0:HEAD_DIM, pl.ds(st, tile)] = lax.dot_general(
                eye, v_ref[pl.ds(st, tile), :], (((1,), (1,)), ((), ())),
                preferred_element_type=F32).astype(BF16)
            return carry
        lax.fori_loop(0, seq // tile, transpose_block, 0)
        vt_scr[HEAD_DIM:, :] = jnp.ones((vt_scr.shape[0] - HEAD_DIM, seq), BF16)
        key = lax.broadcasted_iota(jnp.int32, (tile, LANES), 0)
        ln = lax.broadcasted_iota(jnp.int32, (tile, LANES), 1)
        kpos = jnp.where(ln < 3, key // 16, jnp.where(ln < 6, key % 16, 0))
        kpos_scr[...] = kpos.astype(F32).astype(BF16)

    sv = jnp.full((1, LANES), slope, F32)
    s_hi = sv.astype(BF16).astype(F32)
    s_mid = (sv - s_hi).astype(BF16).astype(F32)
    s_lo = (sv - s_hi - s_mid).astype(BF16).astype(F32)
    ln1 = lax.broadcasted_iota(jnp.int32, (SUBLANES, LANES), 1)
    pieces = (s_hi * 16.0, s_mid * 16.0, s_lo * 16.0, s_hi, s_mid, s_lo)
    srow = jnp.zeros((SUBLANES, LANES), F32)
    for idx, piece in enumerate(pieces):
        srow = jnp.where(ln1 == idx, piece, srow)
    srow = jnp.broadcast_to(srow[0:1], (r, LANES)).astype(BF16)

    def stacked_queries(ref):
        q = ref[...].astype(F32) * (LOG2E * DIFF_QK_DIM ** -0.5)
        lane = lax.broadcasted_iota(jnp.int32, q.shape, 1)
        qq = jnp.concatenate([jnp.where(lane < DIFF_QK_DIM, q, 0.0),
                              jnp.where(lane >= DIFF_QK_DIM, q, 0.0)], axis=0).astype(BF16)
        return jnp.concatenate([qq, srow], axis=1)

    col_blocks = [slice(n * ATTN_COL_BLOCK, (n + 1) * ATTN_COL_BLOCK)
                  for n in range(r // ATTN_COL_BLOCK)]

    def score_block(qq_ref, key_tile, slot, cols, diagonal):
        start = pl.multiple_of(key_tile * tile, tile)
        n_keys = cols.start % tile + ATTN_COL_BLOCK if diagonal else tile
        kt = jnp.concatenate([k_ref[pl.ds(start, n_keys), :], kpos_scr[0:n_keys, :]], axis=1)
        s = lax.dot_general(kt, qq_ref[cols, :], (((1,), (1,)), ((), ())),
                            preferred_element_type=F32)
        if diagonal:
            key = lax.broadcasted_iota(jnp.int32, s.shape, 0)
            qry = lax.broadcasted_iota(jnp.int32, s.shape, 1) + cols.start % tile
            s = jnp.where(key <= qry, s, NEG_BIG)
            if n_keys < tile:
                s_scr[slot, n_keys:, cols] = jnp.full((tile - n_keys, ATTN_COL_BLOCK), NEG_BIG, F32)
        s_scr[slot, 0:n_keys, cols] = s
        smax_scr[slot, :, cols] = jnp.max(s, axis=0, keepdims=True)

    def accumulate_block(key_tile, slot, cols, diagonal=False):
        start = pl.multiple_of(key_tile * tile, tile)
        n_keys = cols.start % tile + ATTN_COL_BLOCK if diagonal else tile
        cj = slope * ((key_tile - qi) * tile).astype(F32)
        m_prev = m_scr[:, cols]
        m_new = jnp.maximum(m_prev, smax_scr[slot, :, cols] + cj)
        alpha = jnp.exp2(m_prev - m_new)
        p = jnp.exp2((s_scr[slot, 0:n_keys, cols] - (m_new - cj)).astype(BF16))
        acc_scr[:, cols] = alpha * acc_scr[:, cols] + jnp.dot(
            vt_scr[:, pl.ds(start, n_keys)], p, preferred_element_type=F32)
        m_scr[:, cols] = m_new

    qq_scr[...] = stacked_queries(q_ref)

    @pl.when(qi == 0)
    def _():
        for cols in col_blocks:
            score_block(qq_scr, 0, 0, cols, True)

    m_scr[...] = jnp.full(m_scr.shape, NEG_BIG, F32)
    acc_scr[...] = jnp.zeros(acc_scr.shape, F32)

    first_slot = ((qi * (qi + 1)) // 2) % 2

    for slot in range(2):
        @pl.when(jnp.logical_and(qi > 0, first_slot == slot))
        def _():
            for cols in col_blocks:
                score_block(qq_scr, 0, 1 - slot, cols, False)
                accumulate_block(qi, slot, cols, diagonal=True)

    def body(t, carry):
        for slot in range(2):
            @pl.when((first_slot + t) % 2 == slot)
            def _():
                for cols in col_blocks:
                    score_block(qq_scr, t, 1 - slot, cols, False)
                    accumulate_block(t - 1, slot, cols)
        return carry

    lax.fori_loop(1, qi, body, 0)

    last_tile = jnp.where(qi == 0, 0, qi - 1)
    next_q = jnp.minimum(qi + 1, seq // tile - 1)
    for slot in range(2):
        @pl.when((first_slot + qi) % 2 == slot)
        def _():
            qqn_scr[...] = stacked_queries(qn_ref)
            for cols in col_blocks:
                score_block(qqn_scr, next_q, 1 - slot, cols, True)
                accumulate_block(last_tile, slot, cols)

    lv = lam_ref[...]
    lam_init = li_ref[:, 0:1]
    lam = (jnp.exp(jnp.sum(lv[0:1] * lv[1:2], axis=-1, keepdims=True))
           - jnp.exp(jnp.sum(lv[2:3] * lv[3:4], axis=-1, keepdims=True)) + lam_init)
    acc = acc_scr[...]
    o_all = acc[:HEAD_DIM] * (1.0 / acc[HEAD_DIM:HEAD_DIM + 1])
    o = (o_all[:, :tile] - lam * o_all[:, tile:]).T
    o_ref[...] = (_rms(o, nw_ref[...]) * (1.0 - lam_init)).astype(BF16)


def _attn_call(p1, lam_vecs, lam_init_rows, norm_w, layer, batch, tile):
    n_tok = p1.shape[0]
    seq = n_tok // batch
    n_q = seq // tile
    qcol = 3 * BRANCH_WIDTH // HEAD_DIM
    kcol = qcol + N_HEADS
    vcol = kcol + N_HEADS
    return pl.pallas_call(
        functools.partial(_attn_kernel, tile=tile),
        grid=(batch, N_HEADS, n_q),
        in_specs=[pl.BlockSpec((tile, HEAD_DIM), lambda b, h, i: (b * n_q + i, qcol + h)),
                  pl.BlockSpec((tile, HEAD_DIM),
                               lambda b, h, i: (b * n_q + jnp.minimum(i + 1, n_q - 1), qcol + h)),
                  pl.BlockSpec((seq, HEAD_DIM), lambda b, h, i: (b, kcol + h)),
                  pl.BlockSpec((seq, HEAD_DIM), lambda b, h, i: (b, vcol + h)),
                  _layer_spec(lam_vecs, layer), _layer_spec(lam_init_rows, layer),
                  _layer_spec(norm_w, layer)],
        out_specs=pl.BlockSpec((tile, HEAD_DIM), lambda b, h, i: (b * n_q + i, h)),
        out_shape=jax.ShapeDtypeStruct((n_tok, BRANCH_WIDTH), BF16),
        scratch_shapes=[pltpu.VMEM((HEAD_DIM + ONES_ROWS, seq), BF16),
                        pltpu.VMEM((tile, LANES), BF16),
                        pltpu.VMEM((2 * tile, 2 * LANES), BF16),
                        pltpu.VMEM((2 * tile, 2 * LANES), BF16),
                        pltpu.VMEM((2, tile, 2 * tile), F32),
                        pltpu.VMEM((2, 1, 2 * tile), F32),
                        pltpu.VMEM((1, 2 * tile), F32),
                        pltpu.VMEM((HEAD_DIM + ONES_ROWS, 2 * tile), F32)],
        compiler_params=pltpu.CompilerParams(
            dimension_semantics=("arbitrary", "arbitrary", "arbitrary"),
            vmem_limit_bytes=VMEM_LIMIT),
        name="diff_attn",
    )(p1, p1, p1, p1, lam_vecs, lam_init_rows, norm_w)


def _final_kernel(x_ref, xh_ref, og_ref, oa_ref, npre_ref, npost_ref, wg_ref, wm_ref, scw_ref,
                  wb_ref, wo_ref, out_ref, *, tiles_per_seq):
    i = pl.program_id(0)
    w = BRANCH_WIDTH
    x = x_ref[...]
    tile, d = x.shape
    hb = _rms(x, npre_ref[...]).astype(BF16)
    hh = _rms(xh_ref[...], npre_ref[...]).astype(BF16)

    def proj(cols):
        return jnp.dot(hb, wg_ref[:, cols], preferred_element_type=F32)

    cxe = jnp.dot(jnp.concatenate([hb, hh], axis=0), wg_ref[:, w:3 * w], preferred_element_type=F32)
    cxe = cxe[:, :w] * cxe[:, w:]
    cx = cxe[:tile]
    cxh = jnp.where(i % tiles_per_seq == 0, 0.0, cxe[tile:])
    rows = lax.broadcasted_iota(jnp.int32, (tile, w), 0)
    conv = scw_ref[SC_CONV - 1:SC_CONV, :] * cx
    for tap in range(SC_CONV - 1):
        shift = SC_CONV - 1 - tap
        shifted = pltpu.roll(cx, shift, axis=0)
        for r in range(shift):
            src = FINAL_HALO - shift + r
            shifted = jnp.where(rows == r, cxh[src:src + 1, :], shifted)
        conv = conv + scw_ref[tap:tap + 1, :] * shifted
    y_c = proj(slice(0, w)) * conv * _silu(proj(slice(5 * w, 6 * w)))
    y_a = og_ref[...].astype(F32) * _silu(proj(slice(3 * w, 4 * w)))
    y_b = oa_ref[...].astype(F32) * _silu(proj(slice(4 * w, 5 * w)))

    ys = [y.astype(BF16) for y in (y_a, y_b, y_c)]
    out = None
    for c0 in range(0, d, MERGE_COL_BLOCK):
        cols = slice(c0, c0 + MERGE_COL_BLOCK)
        merged = None
        for k, y in enumerate(ys):
            gate = _sigmoid(jnp.dot(hb, wm_ref[:, k * d + c0:k * d + c0 + MERGE_COL_BLOCK],
                                    preferred_element_type=F32))
            term = gate * jnp.dot(y, wb_ref[k, :, cols], preferred_element_type=F32)
            merged = term if merged is None else merged + term
        part = jnp.dot(merged.astype(BF16), wo_ref[cols, :], preferred_element_type=F32)
        out = part if out is None else out + part
    out_ref[...] = x + _rms(out, npost_ref[...])


def _final_call(x, o_gdn, o_attn, norm_pre, norm_post, w_all, sc_w, w_branch, w_out, layer, batch,
                tile):
    n_tok, d = x.shape
    tiles_per_seq = n_tok // batch // tile
    halo_blocks = tile // FINAL_HALO
    const = dict(pipeline_mode=pl.Buffered(1))
    assert MIXER_COLS == GATE_COLS
    return pl.pallas_call(
        functools.partial(_final_kernel, tiles_per_seq=tiles_per_seq),
        grid=(n_tok // tile,),
        in_specs=[pl.BlockSpec((tile, d), lambda i: (i, 0)),
                  pl.BlockSpec((FINAL_HALO, d), lambda i: (jnp.maximum(i * halo_blocks - 1, 0), 0)),
                  pl.BlockSpec((tile, BRANCH_WIDTH), lambda i: (i, 0)),
                  pl.BlockSpec((tile, BRANCH_WIDTH), lambda i: (i, 0)),
                  _layer_spec(norm_pre, layer), _layer_spec(norm_post, layer),
                  pl.BlockSpec((None, d, GATE_COLS), lambda i: (layer, 0, 1), **const),
                  pl.BlockSpec((None, d, GATE_COLS), lambda i: (layer, 0, 2), **const),
                  _layer_spec(sc_w, layer),
                  _layer_spec(w_branch, layer, **const), _layer_spec(w_out, layer, **const)],
        out_specs=pl.BlockSpec((tile, d), lambda i: (i, 0)),
        out_shape=jax.ShapeDtypeStruct((n_tok, d), F32),
        compiler_params=pltpu.CompilerParams(dimension_semantics=("arbitrary",),
                                             vmem_limit_bytes=VMEM_LIMIT),
        name="merge_out",
    )(x, x, o_gdn, o_attn, norm_pre, norm_post, w_all, w_all, sc_w, w_branch, w_out)


def _head_rows(vals):
    return jnp.pad(vals.astype(F32), ((0, 0), (N_HEADS, LANES - 2 * N_HEADS)))[:, None, :]


def kernel(x, norm_pre, norm_post, w_in, gdn_conv_w, gdn_a_log, gdn_dt_bias, gdn_norm_w,
           diff_lambda, diff_norm_w, sc_conv_w, w_branch, w_out):
    bsz, seq, d = x.shape
    depth = w_in.shape[0]
    w_in_t = jnp.swapaxes(w_in, 1, 2)
    w_all = _regroup_call(w_in_t)
    wb = w_branch.astype(BF16)
    wo = w_out.astype(BF16)
    npre = norm_pre[:, None, :]
    npost = norm_post[:, None, :]
    alog_rows = _head_rows(gdn_a_log)
    dtb_rows = _head_rows(gdn_dt_bias)
    gdn_nw = gdn_norm_w[:, None, :]
    diff_nw = diff_norm_w[:, None, :]
    lam_init_rows = jnp.asarray(np.broadcast_to(
        np.array([0.8 - 0.6 * math.exp(-0.3 * l) for l in range(depth)], np.float32)[:, None, None],
        (depth, 1, LANES)))
    xf = x.reshape(bsz * seq, d)
    for l in range(depth):
        p1, ba = _proj_call(xf, npre, w_all, w_in_t, l, min(PROJ_TILE, seq))
        o_gdn = _gdn_call(p1, ba, gdn_conv_w, alog_rows, dtb_rows, gdn_nw, l, bsz,
                          min(GDN_CHUNK, seq))
        o_attn = _attn_call(p1, diff_lambda, lam_init_rows, diff_nw, l, bsz,
                            min(ATTN_TILE, seq))
        xf = _final_call(xf, o_gdn, o_attn, npre, npost, w_all, sc_conv_w, wb, wo, l, bsz,
                         min(FINAL_TILE, seq))
    return xf.reshape(bsz, seq, d)
```
